```python
import jax, jax.numpy as jnp
from jax import lax
import numpy as np

D_MODEL = 2048
BATCH = 16
SEQ = 256
DEPTH = 2
DEC_BATCH = 8
DEC_SEQ = 4096
PAST_LEN = 512

GRID_W = 64
D_CONV = D_MODEL // 2
CONV_W = 3
GLA_HEADS = 4
D_K = D_MODEL // 2
D_V = D_MODEL
DK_HEAD = D_K // GLA_HEADS
DV_HEAD = D_V // GLA_HEADS
GK_RANK = 16
GATE_NORM = 16.0
LOG_DECAY_MIN = -1.0
GLA_CHUNK = 64
N_KEYS = 128
N_EXPERTS = N_KEYS * N_KEYS
PEER_HEADS = 8
PEER_DK = 256
PEER_TOPK = 16
PEER_BLOCK = 128
N_ADA = 6
D_IN = 3 * D_CONV + 2 * D_K + 2 * D_V + 2 * GK_RANK + 2 * D_MODEL
ALPHA = (2.0 * DEPTH) ** 0.25
BETA = (8.0 * DEPTH) ** -0.25
LN_EPS = 1e-5

kernel_name = 'hybrid_conv_gla_peer_diffusion_step'


def layer_norm(x, gain=None, bias=None):
    xf = x.astype(jnp.float32)
    mu = jnp.mean(xf, axis=-1, keepdims=True)
    var = jnp.mean(jnp.square(xf - mu), axis=-1, keepdims=True)
    y = (xf - mu) * lax.rsqrt(var + LN_EPS)
    if gain is not None:
        y = y * gain.astype(jnp.float32) + bias.astype(jnp.float32)
    return y.astype(x.dtype)


def conv3(u, w, axis):
    n = u.shape[axis]
    pad = [(1, 1) if a == axis else (0, 0) for a in range(u.ndim)]
    up = jnp.pad(u, pad)
    return (lax.slice_in_dim(up, 0, n, axis=axis) * w[0]
            + lax.slice_in_dim(up, 1, n + 1, axis=axis) * w[1]
            + lax.slice_in_dim(up, 2, n + 2, axis=axis) * w[2])


def conv_latent(u, w):
    B, T, C = u.shape
    rows = T // GRID_W
    g = u.reshape(B, rows, GRID_W, C)
    half = C // 2
    gh = conv3(g[..., :half], w[:, :half], axis=2)
    gv = conv3(g[..., half:], w[:, half:], axis=1)
    return jnp.concatenate([gh, gv], axis=-1).reshape(B, T, C)


def to_heads(t):
    B, T, W = t.shape
    return t.reshape(B, T, GLA_HEADS, W // GLA_HEADS).transpose(0, 2, 1, 3)


def gla_chunked(q, k, v, g, s0):
    out_dtype = v.dtype
    B, H, T, dk = q.shape
    dv = v.shape[-1]
    n = T // GLA_CHUNK
    f32 = jnp.float32
    q = q.astype(f32).reshape(B, H, n, GLA_CHUNK, dk)
    k = k.astype(f32).reshape(B, H, n, GLA_CHUNK, dk)
    v = v.astype(f32).reshape(B, H, n, GLA_CHUNK, dv)
    b = jnp.cumsum(g.astype(f32).reshape(B, H, n, GLA_CHUNK, dk), axis=3)
    b_last = b[:, :, :, -1:, :]
    qe = q * jnp.exp(b)
    ke = k * jnp.exp(-b)
    kd = k * jnp.exp(b_last - b)
    tril = jnp.tril(jnp.ones((GLA_CHUNK, GLA_CHUNK), dtype=bool))
    att = jnp.where(tril, jnp.einsum('bhncd,bhnsd->bhncs', qe, ke), 0.0)
    o_intra = jnp.einsum('bhncs,bhnse->bhnce', att, v)
    decay = jnp.exp(b_last[:, :, :, 0, :])

    def step(S, xs):
        qe_c, kd_c, v_c, dl = xs
        o_c = jnp.einsum('bhcd,bhde->bhce', qe_c, S)
        S = S * dl[..., :, None] + jnp.einsum('bhcd,bhce->bhde', kd_c, v_c)
        return S, o_c

    xs = (jnp.moveaxis(qe, 2, 0), jnp.moveaxis(kd, 2, 0), jnp.moveaxis(v, 2, 0), jnp.moveaxis(decay, 2, 0))
    S, o_inter = lax.scan(step, s0.astype(f32), xs)
    o = (o_intra + jnp.moveaxis(o_inter, 0, 2)).reshape(B, H, T, dv)
    return o.astype(out_dtype), S


def parallel_mixer(h, latent, s_f0, s_b0, w_in, w_conv, w_a, w_gk_up, b_gk, w_gla_norm, w_b, w_o):
    B, T, _ = h.shape
    z = h @ w_in
    sizes = (D_CONV, D_CONV, D_CONV, D_K, D_K, D_V, D_V, GK_RANK, GK_RANK, D_MODEL, D_MODEL)
    cb, cc, cx, q, k, v, r, lf, lb, ga, gb = jnp.split(z, np.cumsum(sizes)[:-1].tolist(), axis=-1)
    u = cc * cx
    u = conv_latent(u, w_conv) if latent else conv3(u, w_conv, axis=1)
    y_a = (cb * u) @ w_a
    g_f = jnp.maximum(jax.nn.log_sigmoid((lf @ w_gk_up[0] + b_gk[0]).astype(jnp.float32)) / GATE_NORM, LOG_DECAY_MIN)
    g_b = jnp.maximum(jax.nn.log_sigmoid((lb @ w_gk_up[1] + b_gk[1]).astype(jnp.float32)) / GATE_NORM, LOG_DECAY_MIN)
    qh = to_heads(q * (DK_HEAD ** -0.5))
    kh = to_heads(k)
    vh = to_heads(v)
    o_f, s_f = gla_chunked(qh, kh, vh, to_heads(g_f), s_f0)
    flip = lambda t: jnp.flip(t, axis=2)
    o_b, s_b = gla_chunked(flip(qh), flip(kh), flip(vh), flip(to_heads(g_b)), s_b0)
    o = (o_f + flip(o_b)).astype(jnp.float32)
    o = o * lax.rsqrt(jnp.mean(jnp.square(o), axis=-1, keepdims=True) + LN_EPS)
    o = o * w_gla_norm.reshape(GLA_HEADS, 1, DV_HEAD).astype(jnp.float32)
    o = o.astype(h.dtype).transpose(0, 2, 1, 3).reshape(B, T, D_V) * jax.nn.silu(r)
    y_b = o @ w_b
    y = jax.nn.sigmoid(ga) * y_a + jax.nn.sigmoid(gb) * y_b
    return y @ w_o, s_f, s_b


def peer_ffn(h, w_pq, sub_keys, w_u, w_v):
    B, T, D = h.shape
    xb = h.reshape(-1, PEER_BLOCK, D)

    def block(x):
        qq = (x @ w_pq).reshape(PEER_BLOCK, PEER_HEADS, 2, PEER_DK // 2)
        s1 = jnp.einsum('phd,nd->phn', qq[:, :, 0], sub_keys[0])
        s2 = jnp.einsum('phd,nd->phn', qq[:, :, 1], sub_keys[1])
        v1, i1 = lax.top_k(s1, PEER_TOPK)
        v2, i2 = lax.top_k(s2, PEER_TOPK)
        cand = (v1[..., :, None] + v2[..., None, :]).reshape(PEER_BLOCK, PEER_HEADS, PEER_TOPK * PEER_TOPK)
        sv, si = lax.top_k(cand, PEER_TOPK)
        e1 = jnp.take_along_axis(i1, si // PEER_TOPK, axis=-1)
        e2 = jnp.take_along_axis(i2, si % PEER_TOPK, axis=-1)
        idx = e1 * N_KEYS + e2
        gate = jax.nn.softmax(sv.astype(jnp.float32), axis=-1).astype(x.dtype)
        act = jax.nn.gelu(jnp.einsum('pd,phkd->phk', x, w_u[idx]))
        return jnp.einsum('phk,phkd->pd', gate * act, w_v[idx])

    return lax.map(block, xb).reshape(B, T, D)


def trunk_layer(x, cond, latent, s_f0, s_b0, w_in, w_conv, w_a, w_gk_up, b_gk, w_gla_norm, w_b, w_o,
                w_ada, b_ada, ln_g, ln_b, w_pq, peer_keys, peer_u, peer_v):
    mod = jax.nn.silu(cond) @ w_ada + b_ada
    sh1, sc1, g1, sh2, sc2, g2 = jnp.split(mod[:, None, :], N_ADA, axis=-1)
    h = layer_norm(x) * (1.0 + sc1) + sh1
    m, s_f, s_b = parallel_mixer(h, latent, s_f0, s_b0, w_in, w_conv, w_a, w_gk_up, b_gk, w_gla_norm, w_b, w_o)
    x = layer_norm(ALPHA * x + g1 * m, ln_g[0], ln_b[0])
    h = layer_norm(x) * (1.0 + sc2) + sh2
    f = peer_ffn(h, w_pq, peer_keys, peer_u, peer_v)
    x = layer_norm(ALPHA * x + g2 * f, ln_g[1], ln_b[1])
    return x, s_f, s_b


def setup_inputs(seed: int = 0) -> dict:
    key = jax.random.key(seed)
    ks = jax.random.split(key, 24)
    f32 = jnp.float32

    def nrm(k, shape, s):
        return jax.random.normal(k, shape, f32) * s

    return {
        'x_prompt': nrm(ks[0], (BATCH, SEQ, D_MODEL), 1.0),
        'x_sample': nrm(ks[1], (DEC_BATCH, DEC_SEQ, D_MODEL), 1.0),
        'state_gla': nrm(ks[2], (DEC_BATCH, DEPTH, 2, GLA_HEADS, DK_HEAD, DV_HEAD), 0.5),
        'c': nrm(ks[3], (DEC_BATCH, D_MODEL), 1.0),
        'c_ctx': nrm(ks[4], (D_MODEL,), 1.0),
        'w_in': nrm(ks[5], (DEPTH, D_MODEL, D_IN), D_MODEL ** -0.5),
        'w_conv': nrm(ks[6], (DEPTH, CONV_W, D_CONV), CONV_W ** -0.5),
        'w_a': nrm(ks[7], (DEPTH, D_CONV, D_MODEL), D_CONV ** -0.5),
        'w_gk_up': nrm(ks[8], (DEPTH, 2, GK_RANK, D_K), GK_RANK ** -0.5),
        'b_gk': nrm(ks[9], (DEPTH, 2, D_K), 0.1),
        'w_gla_norm': 1.0 + nrm(ks[10], (DEPTH, D_V), 0.02),
        'w_b': nrm(ks[11], (DEPTH, D_V, D_MODEL), D_V ** -0.5),
        'w_o': nrm(ks[12], (DEPTH, D_MODEL, D_MODEL), BETA * D_MODEL ** -0.5),
        'w_ada': nrm(ks[13], (DEPTH, D_MODEL, N_ADA * D_MODEL), 0.5 * D_MODEL ** -0.5),
        'b_ada': nrm(ks[14], (DEPTH, N_ADA * D_MODEL), 0.02),
        'ln_g': 1.0 + nrm(ks[15], (DEPTH, 2, D_MODEL), 0.02),
        'ln_b': nrm(ks[16], (DEPTH, 2, D_MODEL), 0.02),
        'w_pq': nrm(ks[17], (DEPTH, D_MODEL, PEER_HEADS * PEER_DK), D_MODEL ** -0.5),
        'peer_keys': nrm(ks[18], (DEPTH, 2, N_KEYS, PEER_DK // 2), (PEER_DK // 2) ** -0.5),
        'peer_u': nrm(ks[19], (DEPTH, N_EXPERTS, D_MODEL), D_MODEL ** -0.5),
        'peer_v': nrm(ks[20], (DEPTH, N_EXPERTS, D_MODEL), BETA * 0.3),
    }


def reference(x_prompt, x_sample, state_gla, c, c_ctx, w_in, w_conv, w_a, w_gk_up, b_gk, w_gla_norm, w_b, w_o,
              w_ada, b_ada, ln_g, ln_b, w_pq, peer_keys, peer_u, peer_v):
    xp = x_prompt
    xs = x_sample
    zero_state = jnp.zeros((x_prompt.shape[0], GLA_HEADS, DK_HEAD, DV_HEAD), jnp.float32)
    cond_ctx = c_ctx[None, :]
    ctx_states = []
    for l in range(DEPTH):
        lw = (w_in[l], w_conv[l], w_a[l], w_gk_up[l], b_gk[l], w_gla_norm[l], w_b[l], w_o[l],
              w_ada[l], b_ada[l], ln_g[l], ln_b[l], w_pq[l], peer_keys[l], peer_u[l], peer_v[l])
        xp, s_f, s_b = trunk_layer(xp, cond_ctx, False, zero_state, zero_state, *lw)
        ctx_states.append(jnp.stack([s_f, s_b], axis=1))
        xs, _, _ = trunk_layer(xs, c, True, state_gla[:, l, 0], state_gla[:, l, 1], *lw)
    new_state_gla = jnp.stack(ctx_states, axis=1).astype(x_prompt.dtype)
    return (xp, xs, new_state_gla)
```

```python
import functools

import jax
import jax.numpy as jnp
from jax import lax
from jax.experimental import pallas as pl
from jax.experimental.pallas import tpu as pltpu

F32 = jnp.float32
BF16 = jnp.bfloat16

LN_EPS = 1e-5
GATE_NORM = 16.0
LOG_DECAY_MIN = -1.0
GLA_CHUNK = 64
GLA_HEADS = 4
GRID_W = 64
PEER_HEADS = 8
PEER_TOPK = 16
N_ADA = 6

V7X_LANES = 128
V7X_SUBLANES = 8
V7X_VMEM_LIMIT_BYTES = 56 * 1024 * 1024

ROW_TILE = 512
IN_PROJ_COL_TILE = 1920
MERGE_COL_TILE = 1024
CONV_CH_TILE = 256
GLA_STEP_ROWS = 256
ROUTER_TOK_TILE = 256
PEER_TOK_TILE = 512
PEER_EXPERT_TILE = 1024
ADA_COL_TILE = 1024


def _params(n_grid_dims):
    return pltpu.CompilerParams(
        dimension_semantics=("arbitrary",) * n_grid_dims,
        vmem_limit_bytes=V7X_VMEM_LIMIT_BYTES)


def _sigmoid(x):
    return 1.0 / (1.0 + jnp.exp(-x))


def _gelu_tanh(x):
    return 0.5 * x * (1.0 + jnp.tanh(0.7978845608028654 * (x + 0.044715 * (x * x * x))))


def _log_sigmoid(x):
    return jnp.minimum(x, 0.0) - jnp.log(1.0 + jnp.exp(-jnp.abs(x)))


def _ln(x):
    mu = jnp.mean(x, axis=-1, keepdims=True)
    xc = x - mu
    var = jnp.mean(xc * xc, axis=-1, keepdims=True)
    return xc * lax.rsqrt(var + LN_EPS)


def _dot(a, b):
    return jnp.dot(a, b, preferred_element_type=F32)


def _dot_nt(a, b):
    return lax.dot_general(a, b, (((1,), (1,)), ((), ())), preferred_element_type=F32)


def _dot_tn(a, b):
    return lax.dot_general(a, b, (((0,), (0,)), ((), ())), preferred_element_type=F32)


def _ada_body(c_ref, w_ref, b_ref, o_ref):
    c = c_ref[...]
    s = (c * _sigmoid(c)).astype(BF16)
    o_ref[...] = _dot(s, w_ref[...].astype(BF16)) + b_ref[...]


def _ada_mod(cond_pad, w_ada, b_ada):
    rows, d = cond_pad.shape
    n_out = w_ada.shape[1]
    tn = ADA_COL_TILE
    return pl.pallas_call(
        _ada_body,
        grid=(n_out // tn,),
        in_specs=[pl.BlockSpec((rows, d), lambda j: (0, 0)),
                  pl.BlockSpec((d, tn), lambda j: (0, j)),
                  pl.BlockSpec((1, tn), lambda j: (0, j))],
        out_specs=pl.BlockSpec((rows, tn), lambda j: (0, j)),
        out_shape=jax.ShapeDtypeStruct((rows, n_out), F32),
        compiler_params=_params(1),
        name="ada_mod",
    )(cond_pad, w_ada, b_ada.reshape(1, n_out))


def _in_proj_body(x_ref, mod_ref, w_ref, z_ref, h_scr):
    @pl.when(pl.program_id(1) == 0)
    def _():
        m = mod_ref[0]
        h = _ln(x_ref[...]) * (1.0 + m[1:2]) + m[0:1]
        h_scr[...] = h.astype(BF16)

    z_ref[...] = _dot(h_scr[...], w_ref[...])


def _in_proj(x, mod_t, w_in_r):
    n, d = x.shape
    nc = w_in_r.shape[1]
    tm, tn = ROW_TILE, IN_PROJ_COL_TILE
    return pl.pallas_call(
        _in_proj_body,
        grid=(n // tm, nc // tn),
        in_specs=[pl.BlockSpec((tm, d), lambda i, j: (i, 0)),
                  pl.BlockSpec((1, V7X_SUBLANES, d), lambda i, j: (i, 0, 0)),
                  pl.BlockSpec((d, tn), lambda i, j: (0, j))],
        out_specs=pl.BlockSpec((tm, tn), lambda i, j: (i, j)),
        out_shape=jax.ShapeDtypeStruct((n, nc), F32),
        scratch_shapes=[pltpu.VMEM((tm, d), BF16)],
        compiler_params=_params(2),
        name="in_proj",
    )(x, mod_t, w_in_r)


def _conv3_rows(u, w, shift, period):
    n = u.shape[0]
    pos = lax.broadcasted_iota(jnp.int32, u.shape, 0) % period
    prev = jnp.where(pos >= shift, pltpu.roll(u, shift, axis=0), 0.0)
    nxt = jnp.where(pos < period - shift, pltpu.roll(u, n - shift, axis=0), 0.0)
    return prev * w[0:1] + u * w[1:2] + nxt * w[2:3]


def _conv_body(cb_ref, cc_ref, cx_ref, w_ref, p_ref, *, row_len, n_horizontal_tiles):
    u = cc_ref[...] * cx_ref[...]
    w = w_ref[...]
    n = u.shape[0]

    @pl.when(pl.program_id(1) < n_horizontal_tiles)
    def _():
        p_ref[...] = (cb_ref[...] * _conv3_rows(u, w, 1, row_len)).astype(BF16)

    @pl.when(pl.program_id(1) >= n_horizontal_tiles)
    def _():
        p_ref[...] = (cb_ref[...] * _conv3_rows(u, w, row_len, n)).astype(BF16)


def _conv_mixer(z, w_conv, *, row0, n_seq, seq_len, row_len, latent):
    d_conv = w_conv.shape[1]
    ct = CONV_CH_TILE
    n_ct = d_conv // ct
    blk0 = row0 // seq_len
    n_h = (n_ct // 2) if latent else n_ct
    body = functools.partial(_conv_body, row_len=row_len, n_horizontal_tiles=n_h)
    return pl.pallas_call(
        body,
        grid=(n_seq, n_ct),
        in_specs=[pl.BlockSpec((seq_len, ct), lambda b, c: (blk0 + b, c)),
                  pl.BlockSpec((seq_len, ct), lambda b, c: (blk0 + b, n_ct + c)),
                  pl.BlockSpec((seq_len, ct), lambda b, c: (blk0 + b, 2 * n_ct + c)),
                  pl.BlockSpec((3, ct), lambda b, c: (0, c))],
        out_specs=pl.BlockSpec((seq_len, ct), lambda b, c: (b, c)),
        out_shape=jax.ShapeDtypeStruct((n_seq * seq_len, d_conv), BF16),
        compiler_params=_params(2),
        name="conv_latent" if latent else "conv_context",
    )(z, z, z, w_conv)


def _split3_dot(tri, g):
    g1 = g.astype(BF16)
    r1 = g - g1.astype(F32)
    g2 = r1.astype(BF16)
    g3 = (r1 - g2.astype(F32)).astype(BF16)
    return _dot(tri, g1) + _dot(tri, g2) + _dot(tri, g3)


def _gla_direction(q_ref, k_ref, v_ref, l_ref, wup, bgk, o_ref, s_scr, *, backward, q_scale):
    rows = q_ref.shape[0]
    n_chunks = rows // GLA_CHUNK
    t_idx = lax.broadcasted_iota(jnp.int32, (GLA_CHUNK, GLA_CHUNK), 0)
    s_idx = lax.broadcasted_iota(jnp.int32, (GLA_CHUNK, GLA_CHUNK), 1)
    keep = (s_idx >= t_idx) if backward else (s_idx <= t_idx)
    tri = jnp.where(keep, 1.0, 0.0).astype(BF16)

    gpre = _dot(l_ref[...].astype(BF16), wup) + bgk
    g = jnp.maximum(_log_sigmoid(gpre) * (1.0 / GATE_NORM), LOG_DECAY_MIN)

    order = range(n_chunks - 1, -1, -1) if backward else range(n_chunks)
    for c in order:
        r = slice(c * GLA_CHUNK, (c + 1) * GLA_CHUNK)
        b = _split3_dot(tri, g[r])
        b_last = b[0:1] if backward else b[GLA_CHUNK - 1:GLA_CHUNK]
        q = q_ref[r, :] * q_scale
        k = k_ref[r, :]
        v = v_ref[r, :].astype(BF16)
        qe = (q * jnp.exp(b)).astype(BF16)
        ke = (k * jnp.exp(-b)).astype(BF16)
        kd = (k * jnp.exp(b_last - b)).astype(BF16)
        att = jnp.where(keep, _dot_nt(qe, ke), 0.0).astype(BF16)
        s = s_scr[...]
        o_ref[r, :] = _dot(att, v) + _dot_nt(qe, s.astype(BF16))
        s_scr[...] = s * jnp.exp(b_last) + _dot_tn(v, kd)


def _gla_body(*refs, zero_init, q_scale):
    if zero_init:
        (qf, kf, vf, lf, qb, kb, vb, lb, wup_ref, bgk_ref,
         of_ref, ob_ref, sfin_ref, sf_scr, sb_scr) = refs
    else:
        (qf, kf, vf, lf, qb, kb, vb, lb, wup_ref, bgk_ref, s0_ref,
         of_ref, ob_ref, sf_scr, sb_scr) = refs
    step = pl.program_id(2)

    @pl.when(step == 0)
    def _():
        if zero_init:
            sf_scr[...] = jnp.zeros_like(sf_scr)
            sb_scr[...] = jnp.zeros_like(sb_scr)
        else:
            sf_scr[...] = s0_ref[0, 0, 0]
            sb_scr[...] = s0_ref[0, 1, 0]

    _gla_direction(qf, kf, vf, lf, wup_ref[0], bgk_ref[0], of_ref, sf_scr,
                   backward=False, q_scale=q_scale)
    _gla_direction(qb, kb, vb, lb, wup_ref[1], bgk_ref[1], ob_ref, sb_scr,
                   backward=True, q_scale=q_scale)

    if zero_init:
        @pl.when(step == pl.num_programs(2) - 1)
        def _():
            sfin_ref[0, 0, 0] = sf_scr[...]
            sfin_ref[0, 1, 0] = sb_scr[...]


def _gla(z, wup_pad, bgk, s0_t, *, row0, n_seq, seq_len, d_k, d_v, cols):
    dk, dv = d_k // GLA_HEADS, d_v // GLA_HEADS
    rs = min(GLA_STEP_ROWS, seq_len)
    steps = seq_len // rs
    blk0 = row0 // rs
    q_blk, k_blk, v_blk, l_blk = (cols["q"] // dk, cols["k"] // dk, cols["v"] // dv,
                                  cols["lfb"] // V7X_LANES)
    zero_init = s0_t is None

    def fwd(col_blk):
        return lambda b, h, s: (blk0 + b * steps + s, col_blk(h))

    def bwd(col_blk):
        return lambda b, h, s: (blk0 + b * steps + (steps - 1 - s), col_blk(h))

    def dir_specs(mk):
        return [pl.BlockSpec((rs, dk), mk(lambda h: q_blk + h)),
                pl.BlockSpec((rs, dk), mk(lambda h: k_blk + h)),
                pl.BlockSpec((rs, dv), mk(lambda h: v_blk + h)),
                pl.BlockSpec((rs, V7X_LANES), mk(lambda h: l_blk))]

    in_specs = dir_specs(fwd) + dir_specs(bwd) + [
        pl.BlockSpec((2, V7X_LANES, dk), lambda b, h, s: (0, 0, h)),
        pl.BlockSpec((2, 1, dk), lambda b, h, s: (0, 0, h))]
    args = [z] * 8 + [wup_pad, bgk]
    n_rows = n_seq * seq_len
    out_shape = [jax.ShapeDtypeStruct((n_rows, d_v), F32)] * 2
    out_specs = [pl.BlockSpec((rs, dv), lambda b, h, s: (b * steps + s, h)),
                 pl.BlockSpec((rs, dv), lambda b, h, s: (b * steps + (steps - 1 - s), h))]
    state_spec = pl.BlockSpec((1, 2, 1, dv, dk), lambda b, h, s: (b, 0, h, 0, 0))
    if zero_init:
        out_shape.append(jax.ShapeDtypeStruct((n_seq, 2, GLA_HEADS, dv, dk), F32))
        out_specs.append(state_spec)
    else:
        in_specs.append(state_spec)
        args.append(s0_t)
    body = functools.partial(_gla_body, zero_init=zero_init, q_scale=float(dk) ** -0.5)
    return pl.pallas_call(
        body,
        grid=(n_seq, GLA_HEADS, steps),
        in_specs=in_specs,
        out_specs=out_specs,
        out_shape=out_shape,
        scratch_shapes=[pltpu.VMEM((dv, dk), F32), pltpu.VMEM((dv, dk), F32)],
        compiler_params=_params(3),
        name="gla_context" if zero_init else "gla_latent",
    )(*args)


def _gla_post_body(of_ref, ob_ref, r_ref, w_ref, o_ref):
    o = of_ref[...] + ob_ref[...]
    o = o * lax.rsqrt(jnp.mean(o * o, axis=-1, keepdims=True) + LN_EPS) * w_ref[...]
    r = r_ref[...]
    o_ref[...] = (o * (r * _sigmoid(r))).astype(BF16)


def _gla_post(o_f, o_b, z, w_norm, *, row0, r_col):
    n_rows, d_v = o_f.shape
    dv = d_v // GLA_HEADS
    tm = min(ROW_TILE, n_rows)
    blk0 = row0 // tm
    r_blk = r_col // dv
    return pl.pallas_call(
        _gla_post_body,
        grid=(n_rows // tm, GLA_HEADS),
        in_specs=[pl.BlockSpec((tm, dv), lambda i, h: (i, h)),
                  pl.BlockSpec((tm, dv), lambda i, h: (i, h)),
                  pl.BlockSpec((tm, dv), lambda i, h: (blk0 + i, r_blk + h)),
                  pl.BlockSpec((1, dv), lambda i, h: (0, h))],
        out_specs=pl.BlockSpec((tm, dv), lambda i, h: (i, h)),
        out_shape=jax.ShapeDtypeStruct((n_rows, d_v), BF16),
        compiler_params=_params(2),
        name="gla_post",
    )(o_f, o_b, z, w_norm.reshape(1, d_v))


def _merge_body(p_ref, og_ref, ga_ref, gb_ref, wa_ref, wb_ref, y_ref):
    y_a = _dot(p_ref[...], wa_ref[...])
    y_b = _dot(og_ref[...], wb_ref[...])
    y_ref[...] = (_sigmoid(ga_ref[...]) * y_a + _sigmoid(gb_ref[...]) * y_b).astype(BF16)


def _merge(p, og, z, w_a, w_b, *, ga_col, gb_col):
    n, d_conv = p.shape
    d_v = og.shape[1]
    d = w_a.shape[1]
    tm, tn = ROW_TILE, MERGE_COL_TILE
    ga_blk, gb_blk = ga_col // tn, gb_col // tn
    return pl.pallas_call(
        _merge_body,
        grid=(n // tm, d // tn),
        in_specs=[pl.BlockSpec((tm, d_conv), lambda i, j: (i, 0)),
                  pl.BlockSpec((tm, d_v), lambda i, j: (i, 0)),
                  pl.BlockSpec((tm, tn), lambda i, j: (i, ga_blk + j)),
                  pl.BlockSpec((tm, tn), lambda i, j: (i, gb_blk + j)),
                  pl.BlockSpec((d_conv, tn), lambda i, j: (0, j)),
                  pl.BlockSpec((d_v, tn), lambda i, j: (0, j))],
        out_specs=pl.BlockSpec((tm, tn), lambda i, j: (i, j)),
        out_shape=jax.ShapeDtypeStruct((n, d), BF16),
        compiler_params=_params(2),
        name="merge",
    )(p, og, z, z, w_a, w_b)


def _out_proj_body(y_ref, x_ref, mod_ref, wo_ref, lg_ref, lb_ref, x1_ref, ht_ref, *, alpha):
    m = mod_ref[0]
    mix = _dot(y_ref[...], wo_ref[...])
    x1 = _ln(alpha * x_ref[...] + m[2:3] * mix) * lg_ref[...] + lb_ref[...]
    x1_ref[...] = x1
    h = _ln(x1) * (1.0 + m[4:5]) + m[3:4]
    ht_ref[...] = h.T.astype(BF16)


def _out_proj(y, x, mod_t, w_o, ln_g, ln_b, *, alpha):
    n, d = x.shape
    tm = ROW_TILE
    return pl.pallas_call(
        functools.partial(_out_proj_body, alpha=alpha),
        grid=(n // tm,),
        in_specs=[pl.BlockSpec((tm, d), lambda i: (i, 0)),
                  pl.BlockSpec((tm, d), lambda i: (i, 0)),
                  pl.BlockSpec((1, V7X_SUBLANES, d), lambda i: (i, 0, 0)),
                  pl.BlockSpec((d, d), lambda i: (0, 0)),
                  pl.BlockSpec((1, d), lambda i: (0, 0)),
                  pl.BlockSpec((1, d), lambda i: (0, 0))],
        out_specs=[pl.BlockSpec((tm, d), lambda i: (i, 0)),
                   pl.BlockSpec((d, tm), lambda i: (0, i))],
        out_shape=[jax.ShapeDtypeStruct((n, d), F32),
                   jax.ShapeDtypeStruct((d, n), BF16)],
        compiler_params=_params(1),
        name="out_proj",
    )(y, x, mod_t, w_o, ln_g.reshape(1, d), ln_b.reshape(1, d))


def _top_values(a, k):
    vals = []
    for _ in range(k):
        m = jnp.max(a, axis=0, keepdims=True)
        vals.append(jnp.maximum(m, 0.0))
        a = jnp.where(a == m, -1.0, a)
    return jnp.concatenate(vals, axis=0)


def _router_body(ht_ref, wpq_ref, keys_ref, a1_ref, a2_ref, th_ref, qq_scr, *, half_dim):
    qq_scr[...] = _dot(wpq_ref[...], ht_ref[...])
    k1 = keys_ref[0]
    k2 = keys_ref[1]
    k = PEER_TOPK

    def head(h, carry):
        base = pl.multiple_of(h * (2 * half_dim), 2 * half_dim)
        q1 = qq_scr[pl.ds(base, half_dim), :].astype(BF16)
        q2 = qq_scr[pl.ds(base + half_dim, half_dim), :].astype(BF16)
        s1 = _dot(k1, q1)
        s2 = _dot(k2, q2)
        a1 = jnp.exp(s1 - jnp.max(s1, axis=0, keepdims=True))
        a2 = jnp.exp(s2 - jnp.max(s2, axis=0, keepdims=True))
        v1 = _top_values(a1, k)
        v2 = _top_values(a2, k)
        cand = jnp.concatenate([v1[a:a + 1] * v2 for a in range(k)], axis=0)
        c = cand
        for _ in range(k - 1):
            m = jnp.max(c, axis=0, keepdims=True)
            c = jnp.where(c == m, -1.0, c)
        tau = jnp.maximum(jnp.max(c, axis=0, keepdims=True), 0.0)
        sel = cand >= tau
        inv_z = 1.0 / jnp.sum(jnp.where(sel, cand, 0.0), axis=0, keepdims=True)
        v2n = v2 * inv_z
        cand_n = jnp.concatenate([v1[a:a + 1] * v2n for a in range(k)], axis=0)
        theta = jnp.min(jnp.where(sel, cand_n, jnp.inf), axis=0, keepdims=True)
        a1_ref[h] = a1
        a2_ref[h] = a2 * inv_z
        th_ref[h] = theta
        return carry

    lax.fori_loop(0, PEER_HEADS, head, 0)


def _router(ht, wpq_t, keys):
    d, n = ht.shape
    n_keys, half_dim = keys.shape[1], keys.shape[2]
    tt = ROUTER_TOK_TILE
    return pl.pallas_call(
        functools.partial(_router_body, half_dim=half_dim),
        grid=(n // tt,),
        in_specs=[pl.BlockSpec((d, tt), lambda i: (0, i)),
                  pl.BlockSpec(wpq_t.shape, lambda i: (0, 0)),
                  pl.BlockSpec(keys.shape, lambda i: (0, 0, 0))],
        out_specs=[pl.BlockSpec((PEER_HEADS, n_keys, tt), lambda i: (0, 0, i)),
                   pl.BlockSpec((PEER_HEADS, n_keys, tt), lambda i: (0, 0, i)),
                   pl.BlockSpec((PEER_HEADS, 1, tt), lambda i: (0, 0, i))],
        out_shape=[jax.ShapeDtypeStruct((PEER_HEADS, n_keys, n), F32),
                   jax.ShapeDtypeStruct((PEER_HEADS, n_keys, n), F32),
                   jax.ShapeDtypeStruct((PEER_HEADS, 1, n), F32)],
        scratch_shapes=[pltpu.VMEM((wpq_t.shape[0], tt), F32)],
        compiler_params=_params(1),
        name="peer_router",
    )(ht, wpq_t, keys)


def _experts_body(ht_ref, a1_ref, a2_ref, th_ref, wu_ref, wvt_ref, ft_ref, gc_scr, *, n_keys):
    e = pl.program_id(1)
    u = _dot(wu_ref[...], ht_ref[...])
    rows_per_step = u.shape[0] // n_keys
    for ii in range(rows_per_step):
        r = slice(ii * n_keys, (ii + 1) * n_keys)
        coef = None
        for h in range(PEER_HEADS):
            p = a1_ref[h, ii:ii + 1, :] * a2_ref[h]
            term = jnp.where(p >= th_ref[h], p, 0.0)
            coef = term if coef is None else coef + term
        gc_scr[r, :] = (coef * _gelu_tanh(u[r])).astype(BF16)
    contrib = _dot(wvt_ref[...], gc_scr[...])

    @pl.when(e == 0)
    def _():
        ft_ref[...] = contrib

    @pl.when(e != 0)
    def _():
        ft_ref[...] += contrib


def _experts(ht, a1, a2, th, w_u, w_vt):
    d, n = ht.shape
    n_exp = w_u.shape[0]
    n_keys = a1.shape[1]
    tt, te = PEER_TOK_TILE, PEER_EXPERT_TILE
    rows_per_step = te // n_keys
    return pl.pallas_call(
        functools.partial(_experts_body, n_keys=n_keys),
        grid=(n // tt, n_exp // te),
        in_specs=[pl.BlockSpec((d, tt), lambda i, e: (0, i)),
                  pl.BlockSpec((PEER_HEADS, rows_per_step, tt), lambda i, e: (0, e, i)),
                  pl.BlockSpec((PEER_HEADS, n_keys, tt), lambda i, e: (0, 0, i)),
                  pl.BlockSpec((PEER_HEADS, 1, tt), lambda i, e: (0, 0, i)),
                  pl.BlockSpec((te, d), lambda i, e: (e, 0)),
                  pl.BlockSpec((d, te), lambda i, e: (0, e))],
        out_specs=pl.BlockSpec((d, tt), lambda i, e: (0, i)),
        out_shape=jax.ShapeDtypeStruct((d, n), F32),
        scratch_shapes=[pltpu.VMEM((te, tt), BF16)],
        compiler_params=_params(2),
        name="peer_experts",
    )(ht, a1, a2, th, w_u, w_vt)


def _ffn_out_body(ft_ref, x1_ref, mod_ref, lg_ref, lb_ref, o_ref, *, alpha):
    m = mod_ref[0]
    f = ft_ref[...].T
    o_ref[...] = _ln(alpha * x1_ref[...] + m[5:6] * f) * lg_ref[...] + lb_ref[...]


def _ffn_out(ft, x1, mod_t, ln_g, ln_b, *, alpha):
    n, d = x1.shape
    tm = ROW_TILE
    return pl.pallas_call(
        functools.partial(_ffn_out_body, alpha=alpha),
        grid=(n // tm,),
        in_specs=[pl.BlockSpec((d, tm), lambda i: (0, i)),
                  pl.BlockSpec((tm, d), lambda i: (i, 0)),
                  pl.BlockSpec((1, V7X_SUBLANES, d), lambda i: (i, 0, 0)),
                  pl.BlockSpec((1, d), lambda i: (0, 0)),
                  pl.BlockSpec((1, d), lambda i: (0, 0))],
        out_specs=pl.BlockSpec((tm, d), lambda i: (i, 0)),
        out_shape=jax.ShapeDtypeStruct((n, d), F32),
        compiler_params=_params(1),
        name="ffn_out",
    )(ft, x1, mod_t, ln_g.reshape(1, d), ln_b.reshape(1, d))


def kernel(x_prompt, x_sample, state_gla, c, c_ctx, w_in, w_conv, w_a, w_gk_up, b_gk, w_gla_norm,
           w_b, w_o, w_ada, b_ada, ln_g, ln_b, w_pq, peer_keys, peer_u, peer_v):
    n_b, seq, d = x_prompt.shape
    dec_b, dec_seq, _ = x_sample.shape
    depth = w_in.shape[0]
    d_conv = w_conv.shape[2]
    d_k = b_gk.shape[2]
    d_v = w_b.shape[1]
    gk_rank = w_gk_up.shape[2]
    n_ctx, n_lat = n_b * seq, dec_b * dec_seq
    n = n_ctx + n_lat
    alpha = (2.0 * depth) ** 0.25
    assert n_ctx % ROW_TILE == 0 and dec_seq % ROW_TILE == 0 and n_ctx % dec_seq == 0
    assert seq % GLA_CHUNK == 0 and dec_seq % GLA_STEP_ROWS == 0 and dec_seq % GRID_W == 0
    assert 2 * gk_rank <= V7X_LANES

    main = 3 * d_conv + 2 * d_k + 2 * d_v
    cols = {"cb": 0, "q": 3 * d_conv, "k": 3 * d_conv + d_k, "v": 3 * d_conv + 2 * d_k,
            "r": 3 * d_conv + 2 * d_k + d_v, "ga": main, "gb": main + d, "lfb": main + 2 * d}
    n_cols = cols["lfb"] + V7X_LANES
    assert n_cols % IN_PROJ_COL_TILE == 0

    x = jnp.concatenate([x_prompt.reshape(n_ctx, d), x_sample.reshape(n_lat, d)], axis=0)
    cond = jnp.concatenate([c_ctx[None, :], c], axis=0)
    cond_rows = -(-cond.shape[0] // V7X_SUBLANES) * V7X_SUBLANES
    cond_pad = jnp.pad(cond, ((0, cond_rows - cond.shape[0]), (0, 0)))
    tile_row0 = jnp.arange(n // ROW_TILE) * ROW_TILE
    tile_cond = jnp.where(tile_row0 < n_ctx, 0, 1 + (tile_row0 - n_ctx) // dec_seq)

    ctx_states = []
    for l in range(depth):
        w = w_in[l]
        w_in_r = jnp.concatenate(
            [w[:, :main], w[:, main + 2 * gk_rank:], w[:, main:main + 2 * gk_rank],
             jnp.zeros((d, V7X_LANES - 2 * gk_rank), w.dtype)], axis=1).astype(BF16)
        wup_pad = jnp.zeros((2, V7X_LANES, d_k), F32)
        wup_pad = wup_pad.at[0, :gk_rank].set(w_gk_up[l, 0]).at[1, gk_rank:2 * gk_rank].set(w_gk_up[l, 1])
        wup_pad = wup_pad.astype(BF16)
        bgk = b_gk[l].reshape(2, 1, d_k)

        mod = _ada_mod(cond_pad, w_ada[l], b_ada[l]).reshape(cond_rows, N_ADA, d)
        mod_t = jnp.pad(mod[tile_cond], ((0, 0), (0, V7X_SUBLANES - N_ADA), (0, 0)))

        z = _in_proj(x, mod_t, w_in_r)

        p = jnp.concatenate([
            _conv_mixer(z, w_conv[l], row0=0, n_seq=n_b, seq_len=seq, row_len=seq, latent=False),
            _conv_mixer(z, w_conv[l], row0=n_ctx, n_seq=dec_b, seq_len=dec_seq, row_len=GRID_W,
                        latent=True)], axis=0)

        of_c, ob_c, s_fin = _gla(z, wup_pad, bgk, None, row0=0, n_seq=n_b, seq_len=seq,
                                 d_k=d_k, d_v=d_v, cols=cols)
        s0_t = jnp.swapaxes(state_gla[:, l], -1, -2)
        of_l, ob_l = _gla(z, wup_pad, bgk, s0_t, row0=n_ctx, n_seq=dec_b, seq_len=dec_seq,
                          d_k=d_k, d_v=d_v, cols=cols)
        ctx_states.append(jnp.swapaxes(s_fin, -1, -2))
        og = jnp.concatenate([
            _gla_post(of_c, ob_c, z, w_gla_norm[l], row0=0, r_col=cols["r"]),
            _gla_post(of_l, ob_l, z, w_gla_norm[l], row0=n_ctx, r_col=cols["r"])], axis=0)

        y = _merge(p, og, z, w_a[l].astype(BF16), w_b[l].astype(BF16),
                   ga_col=cols["ga"], gb_col=cols["gb"])
        x1, ht = _out_proj(y, x, mod_t, w_o[l].astype(BF16), ln_g[l, 0], ln_b[l, 0], alpha=alpha)

        a1, a2, th = _router(ht, w_pq[l].T.astype(BF16), peer_keys[l].astype(BF16))
        ft = _experts(ht, a1, a2, th, peer_u[l].astype(BF16), peer_v[l].T.astype(BF16))
        x = _ffn_out(ft, x1, mod_t, ln_g[l, 1], ln_b[l, 1], alpha=alpha)

    new_state = jnp.stack(ctx_states, axis=1).astype(x_prompt.dtype)
    return (x[:n_ctx].reshape(n_b, seq, d), x[n_ctx:].reshape(dec_b, dec_seq, d), new_state)
```

```python
import functools

import jax
import jax.numpy as jnp
from jax import lax
from jax.experimental import pallas as pl
from jax.experimental.pallas import tpu as pltpu

F32 = jnp.float32
BF16 = jnp.bfloat16

LN_EPS = 1e-5
GATE_NORM = 16.0
LOG_DECAY_MIN = -1.0
GLA_CHUNK = 64
GLA_HEADS = 4
GRID_W = 64
PEER_HEADS = 8
PEER_TOPK = 16
N_ADA = 6

V7X_LANES = 128
V7X_SUBLANES = 8
V7X_BF16_ROWS = 16
V7X_VMEM_LIMIT_BYTES = 56 * 1024 * 1024

ROW_TILE = 512
IN_PROJ_COL_TILE = 1920
MERGE_COL_TILE = 1024
CONV_CH_TILE = 256
GLA_STEP_ROWS = 256
ROUTER_TOK_TILE = 256
PEER_TOK_TILE = 512
PEER_EXPERT_TILE = 1024
ADA_COL_TILE = 1024


def _params(n_grid_dims):
    return pltpu.CompilerParams(
        dimension_semantics=("arbitrary",) * n_grid_dims,
        vmem_limit_bytes=V7X_VMEM_LIMIT_BYTES)


def _sigmoid(x):
    return 1.0 / (1.0 + jnp.exp(-x))


def _gelu_tanh(x):
    return 0.5 * x * (1.0 + jnp.tanh(0.7978845608028654 * (x + 0.044715 * (x * x * x))))


def _log_sigmoid(x):
    return jnp.minimum(x, 0.0) - jnp.log(1.0 + jnp.exp(-jnp.abs(x)))


def _ln(x):
    mu = jnp.mean(x, axis=-1, keepdims=True)
    xc = x - mu
    var = jnp.mean(xc * xc, axis=-1, keepdims=True)
    return xc * lax.rsqrt(var + LN_EPS)


def _dot(a, b):
    return jnp.dot(a, b, preferred_element_type=F32)


def _dot_nt(a, b):
    return lax.dot_general(a, b, (((1,), (1,)), ((), ())), preferred_element_type=F32)


def _dot_tn(a, b):
    return lax.dot_general(a, b, (((0,), (0,)), ((), ())), preferred_element_type=F32)


def _ada_body(c_ref, w_ref, b_ref, o_ref):
    c = c_ref[...]
    s = (c * _sigmoid(c)).astype(BF16)
    o_ref[...] = _dot(s, w_ref[...].astype(BF16)) + b_ref[...]


def _ada_mod(cond_pad, w_ada, b_ada):
    rows, d = cond_pad.shape
    n_out = w_ada.shape[1]
    tn = ADA_COL_TILE
    return pl.pallas_call(
        _ada_body,
        grid=(n_out // tn,),
        in_specs=[pl.BlockSpec((rows, d), lambda j: (0, 0)),
                  pl.BlockSpec((d, tn), lambda j: (0, j)),
                  pl.BlockSpec((1, tn), lambda j: (0, j))],
        out_specs=pl.BlockSpec((rows, tn), lambda j: (0, j)),
        out_shape=jax.ShapeDtypeStruct((rows, n_out), F32),
        compiler_params=_params(1),
        name="ada_mod",
    )(cond_pad, w_ada, b_ada.reshape(1, n_out))


def _in_proj_body(x_ref, mod_ref, w_ref, z_ref, h_scr):
    @pl.when(pl.program_id(1) == 0)
    def _():
        m = mod_ref[0]
        h = _ln(x_ref[...]) * (1.0 + m[1:2]) + m[0:1]
        h_scr[...] = h.astype(BF16)

    z_ref[...] = _dot(h_scr[...], w_ref[...])


def _in_proj(x, mod_t, w_in_r):
    n, d = x.shape
    nc = w_in_r.shape[1]
    tm, tn = ROW_TILE, IN_PROJ_COL_TILE
    return pl.pallas_call(
        _in_proj_body,
        grid=(n // tm, nc // tn),
        in_specs=[pl.BlockSpec((tm, d), lambda i, j: (i, 0)),
                  pl.BlockSpec((1, V7X_SUBLANES, d), lambda i, j: (i, 0, 0)),
                  pl.BlockSpec((d, tn), lambda i, j: (0, j))],
        out_specs=pl.BlockSpec((tm, tn), lambda i, j: (i, j)),
        out_shape=jax.ShapeDtypeStruct((n, nc), F32),
        scratch_shapes=[pltpu.VMEM((tm, d), BF16)],
        compiler_params=_params(2),
        name="in_proj",
    )(x, mod_t, w_in_r)


def _conv3_rows(u, w, shift, period):
    n = u.shape[0]
    pos = lax.broadcasted_iota(jnp.int32, u.shape, 0) % period
    prev = jnp.where(pos >= shift, pltpu.roll(u, shift, axis=0), 0.0)
    nxt = jnp.where(pos < period - shift, pltpu.roll(u, n - shift, axis=0), 0.0)
    return prev * w[0:1] + u * w[1:2] + nxt * w[2:3]


def _conv_body(cb_ref, cc_ref, cx_ref, w_ref, p_ref, *, row_len, n_horizontal_tiles):
    u = cc_ref[...] * cx_ref[...]
    w = w_ref[...]
    n = u.shape[0]

    @pl.when(pl.program_id(1) < n_horizontal_tiles)
    def _():
        p_ref[...] = (cb_ref[...] * _conv3_rows(u, w, 1, row_len)).astype(BF16)

    @pl.when(pl.program_id(1) >= n_horizontal_tiles)
    def _():
        p_ref[...] = (cb_ref[...] * _conv3_rows(u, w, row_len, n)).astype(BF16)


def _conv_mixer(z, w_conv, *, row0, n_seq, seq_len, row_len, latent):
    d_conv = w_conv.shape[1]
    ct = CONV_CH_TILE
    n_ct = d_conv // ct
    blk0 = row0 // seq_len
    n_h = (n_ct // 2) if latent else n_ct
    body = functools.partial(_conv_body, row_len=row_len, n_horizontal_tiles=n_h)
    return pl.pallas_call(
        body,
        grid=(n_seq, n_ct),
        in_specs=[pl.BlockSpec((seq_len, ct), lambda b, c: (blk0 + b, c)),
                  pl.BlockSpec((seq_len, ct), lambda b, c: (blk0 + b, n_ct + c)),
                  pl.BlockSpec((seq_len, ct), lambda b, c: (blk0 + b, 2 * n_ct + c)),
                  pl.BlockSpec((3, ct), lambda b, c: (0, c))],
        out_specs=pl.BlockSpec((seq_len, ct), lambda b, c: (b, c)),
        out_shape=jax.ShapeDtypeStruct((n_seq * seq_len, d_conv), BF16),
        compiler_params=_params(2),
        name="conv_latent" if latent else "conv_context",
    )(z, z, z, w_conv)


def _split2_dot(tri, g):
    g1 = g.astype(BF16)
    g2 = (g - g1.astype(F32)).astype(BF16)
    return _dot(tri, g1) + _dot(tri, g2)


def _gla_direction(q_ref, k_ref, v_ref, l_ref, wup, bgk, o_ref, s_scr, *, backward, q_scale):
    rows = q_ref.shape[0]
    n_chunks = rows // GLA_CHUNK
    shift = GLA_CHUNK.bit_length() - 1
    t_idx = lax.broadcasted_iota(jnp.int32, (rows, rows), 0)
    s_idx = lax.broadcasted_iota(jnp.int32, (rows, rows), 1)
    t_chunk = jnp.right_shift(t_idx, shift)
    s_chunk = jnp.right_shift(s_idx, shift)
    lag = (s_chunk - t_chunk) if backward else (t_chunk - s_chunk)
    causal = (s_idx >= t_idx) if backward else (s_idx <= t_idx)
    tri = jnp.where(lag == 0, jnp.where(causal, 1.0, 0.0), 0.0).astype(BF16)

    gpre = _dot(l_ref[...].astype(BF16), wup) + bgk
    g = jnp.maximum(_log_sigmoid(gpre) * (1.0 / GATE_NORM), LOG_DECAY_MIN)
    yield
    b = _split2_dot(tri, g)
    yield

    order = list(range(n_chunks - 1, -1, -1)) if backward else list(range(n_chunks))
    scan_pos = {c: j for j, c in enumerate(order)}

    def chunk_total(c):
        row = c * GLA_CHUNK if backward else (c + 1) * GLA_CHUNK - 1
        return b[row:row + 1]

    totals = [chunk_total(c) for c in order]
    prefix = [jnp.zeros_like(totals[0])]
    for t in totals[:-1]:
        prefix.append(prefix[-1] + t)
    total = prefix[-1] + totals[-1]

    def per_chunk_rows(fn):
        return jnp.concatenate(
            [jnp.broadcast_to(fn(scan_pos[c]), (GLA_CHUNK, b.shape[1])) for c in range(n_chunks)],
            axis=0)

    q = q_ref[...] * q_scale
    k = k_ref[...]
    v = v_ref[...].astype(BF16)
    qe = q * jnp.exp(b)
    ke = (k * jnp.exp(-b)).astype(BF16)
    kd = k * jnp.exp(per_chunk_rows(lambda j: totals[j]) - b)
    qe_b = qe.astype(BF16)
    kd_b = kd.astype(BF16)
    yield

    att =jnp.where(lag == 0, jnp.where(causal, _dot_nt(qe_b, ke), 0.0), 0.0)
    for dist in range(1, n_chunks):
        if dist == 1:
            q_dist = qe_b
        else:
            between = per_chunk_rows(
                lambda j: jnp.exp(prefix[j] - prefix[j - dist + 1]) if j >= dist
                else jnp.ones_like(total))
            q_dist = (qe * between).astype(BF16)
        att = jnp.where(lag == dist, _dot_nt(q_dist, kd_b), att)

    q_in = (qe * per_chunk_rows(lambda j: jnp.exp(prefix[j]))).astype(BF16)
    k_out = (kd * per_chunk_rows(lambda j: jnp.exp(total - prefix[j] - totals[j]))).astype(BF16)
    yield
    s = s_scr[...]
    o_ref[...] = _dot(att.astype(BF16), v) + _dot_nt(q_in, s.astype(BF16))
    s_scr[...] = s * jnp.exp(total) + _dot_tn(v, k_out)
    yield


def _gla_body(*refs, zero_init, q_scale):
    if zero_init:
        (qf, kf, vf, lf, qb, kb, vb, lb, wup_ref, bgk_ref,
         of_ref, ob_ref, sfin_ref, sf_scr, sb_scr) = refs
    else:
        (qf, kf, vf, lf, qb, kb, vb, lb, wup_ref, bgk_ref, s0_ref,
         of_ref, ob_ref, sf_scr, sb_scr) = refs
    step = pl.program_id(2)

    @pl.when(step == 0)
    def _():
        if zero_init:
            sf_scr[...] = jnp.zeros_like(sf_scr)
            sb_scr[...] = jnp.zeros_like(sb_scr)
        else:
            sf_scr[...] = s0_ref[0, 0, 0]
            sb_scr[...] = s0_ref[0, 1, 0]

    directions = [
        _gla_direction(qf, kf, vf, lf, wup_ref[0], bgk_ref[0], of_ref, sf_scr,
                       backward=False, q_scale=q_scale),
        _gla_direction(qb, kb, vb, lb, wup_ref[1], bgk_ref[1], ob_ref, sb_scr,
                       backward=True, q_scale=q_scale)]
    for _ in zip(*directions):
        pass

    if zero_init:
        @pl.when(step == pl.num_programs(2) - 1)
        def _():
            sfin_ref[0, 0, 0] = sf_scr[...]
            sfin_ref[0, 1, 0] = sb_scr[...]


def _gla(z, wup_pad, bgk, s0_t, *, row0, n_seq, seq_len, d_k, d_v, cols):
    dk, dv = d_k // GLA_HEADS, d_v // GLA_HEADS
    rs = min(GLA_STEP_ROWS, seq_len)
    steps = seq_len // rs
    blk0 = row0 // rs
    q_blk, k_blk, v_blk, l_blk = (cols["q"] // dk, cols["k"] // dk, cols["v"] // dv,
                                  cols["lfb"] // V7X_LANES)
    zero_init = s0_t is None

    def fwd(col_blk):
        return lambda b, h, s: (blk0 + b * steps + s, col_blk(h))

    def bwd(col_blk):
        return lambda b, h, s: (blk0 + b * steps + (steps - 1 - s), col_blk(h))

    def dir_specs(mk):
        return [pl.BlockSpec((rs, dk), mk(lambda h: q_blk + h)),
                pl.BlockSpec((rs, dk), mk(lambda h: k_blk + h)),
                pl.BlockSpec((rs, dv), mk(lambda h: v_blk + h)),
                pl.BlockSpec((rs, V7X_LANES), mk(lambda h: l_blk))]

    in_specs = dir_specs(fwd) + dir_specs(bwd) + [
        pl.BlockSpec((2, V7X_LANES, dk), lambda b, h, s: (0, 0, h)),
        pl.BlockSpec((2, 1, dk), lambda b, h, s: (0, 0, h))]
    args = [z] * 8 + [wup_pad, bgk]
    n_rows = n_seq * seq_len
    out_shape = [jax.ShapeDtypeStruct((n_rows, d_v), F32)] * 2
    out_specs = [pl.BlockSpec((rs, dv), lambda b, h, s: (b * steps + s, h)),
                 pl.BlockSpec((rs, dv), lambda b, h, s: (b * steps + (steps - 1 - s), h))]
    state_spec = pl.BlockSpec((1, 2, 1, dv, dk), lambda b, h, s: (b, 0, h, 0, 0))
    if zero_init:
        out_shape.append(jax.ShapeDtypeStruct((n_seq, 2, GLA_HEADS, dv, dk), F32))
        out_specs.append(state_spec)
    else:
        in_specs.append(state_spec)
        args.append(s0_t)
    body = functools.partial(_gla_body, zero_init=zero_init, q_scale=float(dk) ** -0.5)
    return pl.pallas_call(
        body,
        grid=(n_seq, GLA_HEADS, steps),
        in_specs=in_specs,
        out_specs=out_specs,
        out_shape=out_shape,
        scratch_shapes=[pltpu.VMEM((dv, dk), F32), pltpu.VMEM((dv, dk), F32)],
        compiler_params=_params(3),
        name="gla_context" if zero_init else "gla_latent",
    )(*args)


def _gla_post_body(of_ref, ob_ref, r_ref, w_ref, o_ref):
    o = of_ref[...] + ob_ref[...]
    o = o * lax.rsqrt(jnp.mean(o * o, axis=-1, keepdims=True) + LN_EPS) * w_ref[...]
    r = r_ref[...]
    o_ref[...] = (o * (r * _sigmoid(r))).astype(BF16)


def _gla_post(o_f, o_b, z, w_norm, *, row0, r_col):
    n_rows, d_v = o_f.shape
    dv = d_v // GLA_HEADS
    tm = min(ROW_TILE, n_rows)
    blk0 = row0 // tm
    r_blk = r_col // dv
    return pl.pallas_call(
        _gla_post_body,
        grid=(n_rows // tm, GLA_HEADS),
        in_specs=[pl.BlockSpec((tm, dv), lambda i, h: (i, h)),
                  pl.BlockSpec((tm, dv), lambda i, h: (i, h)),
                  pl.BlockSpec((tm, dv), lambda i, h: (blk0 + i, r_blk + h)),
                  pl.BlockSpec((1, dv), lambda i, h: (0, h))],
        out_specs=pl.BlockSpec((tm, dv), lambda i, h: (i, h)),
        out_shape=jax.ShapeDtypeStruct((n_rows, d_v), BF16),
        compiler_params=_params(2),
        name="gla_post",
    )(o_f, o_b, z, w_norm.reshape(1, d_v))


def _merge_body(p_ref, og_ref, ga_ref, gb_ref, wa_ref, wb_ref, y_ref):
    y_a = _dot(p_ref[...], wa_ref[...])
    y_b = _dot(og_ref[...], wb_ref[...])
    y_ref[...] = (_sigmoid(ga_ref[...]) * y_a + _sigmoid(gb_ref[...]) * y_b).astype(BF16)


def _merge(p, og, z, w_a, w_b, *, ga_col, gb_col):
    n, d_conv = p.shape
    d_v = og.shape[1]
    d = w_a.shape[1]
    tm, tn = ROW_TILE, MERGE_COL_TILE
    ga_blk, gb_blk = ga_col // tn, gb_col // tn
    return pl.pallas_call(
        _merge_body,
        grid=(n // tm, d // tn),
        in_specs=[pl.BlockSpec((tm, d_conv), lambda i, j: (i, 0)),
                  pl.BlockSpec((tm, d_v), lambda i, j: (i, 0)),
                  pl.BlockSpec((tm, tn), lambda i, j: (i, ga_blk + j)),
                  pl.BlockSpec((tm, tn), lambda i, j: (i, gb_blk + j)),
                  pl.BlockSpec((d_conv, tn), lambda i, j: (0, j)),
                  pl.BlockSpec((d_v, tn), lambda i, j: (0, j))],
        out_specs=pl.BlockSpec((tm, tn), lambda i, j: (i, j)),
        out_shape=jax.ShapeDtypeStruct((n, d), BF16),
        compiler_params=_params(2),
        name="merge",
    )(p, og, z, z, w_a, w_b)


def _out_proj_body(y_ref, x_ref, mod_ref, wo_ref, lg_ref, lb_ref, x1_ref, ht_ref, *, alpha):
    m = mod_ref[0]
    mix = _dot(y_ref[...], wo_ref[...])
    x1 = _ln(alpha * x_ref[...] + m[2:3] * mix) * lg_ref[...] + lb_ref[...]
    x1_ref[...] = x1
    h = _ln(x1) * (1.0 + m[4:5]) + m[3:4]
    ht_ref[...] = h.T.astype(BF16)


def _out_proj(y, x, mod_t, w_o, ln_g, ln_b, *, alpha):
    n, d = x.shape
    tm = ROW_TILE
    return pl.pallas_call(
        functools.partial(_out_proj_body, alpha=alpha),
        grid=(n // tm,),
        in_specs=[pl.BlockSpec((tm, d), lambda i: (i, 0)),
                  pl.BlockSpec((tm, d), lambda i: (i, 0)),
                  pl.BlockSpec((1, V7X_SUBLANES, d), lambda i: (i, 0, 0)),
                  pl.BlockSpec((d, d), lambda i: (0, 0)),
                  pl.BlockSpec((1, d), lambda i: (0, 0)),
                  pl.BlockSpec((1, d), lambda i: (0, 0))],
        out_specs=[pl.BlockSpec((tm, d), lambda i: (i, 0)),
                   pl.BlockSpec((d, tm), lambda i: (0, i))],
        out_shape=[jax.ShapeDtypeStruct((n, d), F32),
                   jax.ShapeDtypeStruct((d, n), BF16)],
        compiler_params=_params(1),
        name="out_proj",
    )(y, x, mod_t, w_o, ln_g.reshape(1, d), ln_b.reshape(1, d))


def _top_values(a, k):
    vals = []
    for _ in range(k):
        m = jnp.max(a, axis=0, keepdims=True)
        vals.append(jnp.maximum(m, 0.0))
        a = jnp.where(a == m, -1.0, a)
    return jnp.concatenate(vals, axis=0)


def _router_body(ht_ref, wpq_ref, keys_ref, a1_ref, r_ref, a2_ref, *, half_dim):
    ht = ht_ref[...]
    k1 = keys_ref[0]
    k2 = keys_ref[1]
    k = PEER_TOPK
    sub = V7X_SUBLANES

    def candidates(v1, v2):
        groups = [v1[0:1] * v2[0:sub], v1[0:1] * v2[sub:k]]
        groups += [v1[a:a + 1] * v2[0:sub] for a in range(1, sub)]
        groups.append(v1[sub:k] * v2[0:1])
        return jnp.concatenate(groups, axis=0)

    for h in range(PEER_HEADS):
        qq = _dot(wpq_ref[h * 2 * half_dim:(h + 1) * 2 * half_dim, :], ht)
        q1 = qq[:half_dim].astype(BF16)
        q2 = qq[half_dim:].astype(BF16)
        s1 = _dot(k1, q1)
        s2 = _dot(k2, q2)
        a1 = jnp.exp(s1 - jnp.max(s1, axis=0, keepdims=True))
        a2 = jnp.exp(s2 - jnp.max(s2, axis=0, keepdims=True))
        v1 = _top_values(a1, k)
        v2 = _top_values(a2, k)
        cand = candidates(v1, v2)
        c = cand
        for _ in range(k - 1):
            m = jnp.max(c, axis=0, keepdims=True)
            c = jnp.where(c == m, -1.0, c)
        tau = jnp.maximum(jnp.max(c, axis=0, keepdims=True), 0.0)
        sel = cand >= tau
        inv_z = 1.0 / jnp.sum(jnp.where(sel, cand, 0.0), axis=0, keepdims=True)
        theta = jnp.min(jnp.where(sel, candidates(v1, v2 * inv_z), jnp.inf), axis=0, keepdims=True)
        a1_ref[h] = a1
        r_ref[h] = (theta * (1.0 - 2.0 ** -20)) / a1
        a2_ref[h] = (a2 * inv_z).astype(BF16)


def _router(ht, wpq_t, keys):
    d, n = ht.shape
    n_keys, half_dim = keys.shape[1], keys.shape[2]
    tt = ROUTER_TOK_TILE
    blk = pl.BlockSpec((PEER_HEADS, n_keys, tt), lambda i: (0, 0, i))
    return pl.pallas_call(
        functools.partial(_router_body, half_dim=half_dim),
        grid=(n // tt,),
        in_specs=[pl.BlockSpec((d, tt), lambda i: (0, i)),
                  pl.BlockSpec(wpq_t.shape, lambda i: (0, 0)),
                  pl.BlockSpec(keys.shape, lambda i: (0, 0, 0))],
        out_specs=[blk, blk, blk],
        out_shape=[jax.ShapeDtypeStruct((PEER_HEADS, n_keys, n), F32),
                   jax.ShapeDtypeStruct((PEER_HEADS, n_keys, n), F32),
                   jax.ShapeDtypeStruct((PEER_HEADS, n_keys, n), BF16)],
        compiler_params=_params(1),
        name="peer_router",
    )(ht, wpq_t, keys)


def _experts_body(ht_ref, a1_ref, r_ref, a2_ref, wu_ref, wvt_ref, ft_ref, gc_scr, *, n_keys):
    @pl.when(pl.program_id(1) == 0)
    def _():
        ft_ref[...] = jnp.zeros_like(ft_ref)

    zero = jnp.zeros((), BF16)
    u = _dot(wu_ref[...], ht_ref[...])
    for ii in range(u.shape[0] // n_keys):
        rows = slice(ii * n_keys, (ii + 1) * n_keys)
        coef = None
        for h in range(PEER_HEADS):
            a2 = a2_ref[h]
            bound = r_ref[h, ii:ii + 1, :].astype(BF16)
            weight = a1_ref[h, ii:ii + 1, :].astype(BF16)
            term = jnp.where(a2 >= bound, a2, zero) * weight
            coef = term if coef is None else coef + term
        gc_scr[rows, :] = coef * _gelu_tanh(u[rows]).astype(BF16)
    ft_ref[...] += _dot(wvt_ref[...], gc_scr[...])


def _experts(ht, a1, r, a2, w_u, w_vt):
    d, n = ht.shape
    n_exp = w_u.shape[0]
    n_keys = a1.shape[1]
    tt, te = PEER_TOK_TILE, PEER_EXPERT_TILE
    rows_per_step = te // n_keys
    row_blk = pl.BlockSpec((PEER_HEADS, rows_per_step, tt), lambda i, e: (0, e, i))
    return pl.pallas_call(
        functools.partial(_experts_body, n_keys=n_keys),
        grid=(n // tt, n_exp // te),
        in_specs=[pl.BlockSpec((d, tt), lambda i, e: (0, i)),
                  row_blk, row_blk,
                  pl.BlockSpec((PEER_HEADS, n_keys, tt), lambda i, e: (0, 0, i)),
                  pl.BlockSpec((te, d), lambda i, e: (e, 0)),
                  pl.BlockSpec((d, te), lambda i, e: (0, e))],
        out_specs=pl.BlockSpec((d, tt), lambda i, e: (0, i)),
        out_shape=jax.ShapeDtypeStruct((d, n), F32),
        scratch_shapes=[pltpu.VMEM((te, tt), BF16)],
        compiler_params=_params(2),
        name="peer_experts",
    )(ht, a1, r, a2, w_u, w_vt)


def _ffn_out_body(ft_ref, x1_ref, mod_ref, lg_ref, lb_ref, o_ref, *, alpha):
    m = mod_ref[0]
    f = ft_ref[...].T
    o_ref[...] = _ln(alpha * x1_ref[...] + m[5:6] * f) * lg_ref[...] + lb_ref[...]


def _ffn_out(ft, x1, mod_t, ln_g, ln_b, *, alpha):
    n, d = x1.shape
    tm = ROW_TILE
    return pl.pallas_call(
        functools.partial(_ffn_out_body, alpha=alpha),
        grid=(n // tm,),
        in_specs=[pl.BlockSpec((d, tm), lambda i: (0, i)),
                  pl.BlockSpec((tm, d), lambda i: (i, 0)),
                  pl.BlockSpec((1, V7X_SUBLANES, d), lambda i: (i, 0, 0)),
                  pl.BlockSpec((1, d), lambda i: (0, 0)),
                  pl.BlockSpec((1, d), lambda i: (0, 0))],
        out_specs=pl.BlockSpec((tm, d), lambda i: (i, 0)),
        out_shape=jax.ShapeDtypeStruct((n, d), F32),
        compiler_params=_params(1),
        name="ffn_out",
    )(ft, x1, mod_t, ln_g.reshape(1, d), ln_b.reshape(1, d))


def kernel(x_prompt, x_sample, state_gla, c, c_ctx, w_in, w_conv, w_a, w_gk_up, b_gk, w_gla_norm,
           w_b, w_o, w_ada, b_ada, ln_g, ln_b, w_pq, peer_keys, peer_u, peer_v):
    n_b, seq, d = x_prompt.shape
    dec_b, dec_seq, _ = x_sample.shape
    depth = w_in.shape[0]
    d_conv = w_conv.shape[2]
    d_k = b_gk.shape[2]
    d_v = w_b.shape[1]
    gk_rank = w_gk_up.shape[2]
    n_ctx, n_lat = n_b * seq, dec_b * dec_seq
    n = n_ctx + n_lat
    alpha = (2.0 * depth) ** 0.25
    assert n_ctx % ROW_TILE == 0 and dec_seq % ROW_TILE == 0 and n_ctx % dec_seq == 0
    assert seq % GLA_CHUNK == 0 and dec_seq % GLA_STEP_ROWS == 0 and dec_seq % GRID_W == 0
    assert 2 * gk_rank <= V7X_LANES

    main = 3 * d_conv + 2 * d_k + 2 * d_v
    cols = {"cb": 0, "q": 3 * d_conv, "k": 3 * d_conv + d_k, "v": 3 * d_conv + 2 * d_k,
            "r": 3 * d_conv + 2 * d_k + d_v, "ga": main, "gb": main + d, "lfb": main + 2 * d}
    n_cols = cols["lfb"] + V7X_LANES
    assert n_cols % IN_PROJ_COL_TILE == 0

    x = jnp.concatenate([x_prompt.reshape(n_ctx, d), x_sample.reshape(n_lat, d)], axis=0)
    cond = jnp.concatenate([c_ctx[None, :], c], axis=0)
    cond_rows = -(-cond.shape[0] // V7X_SUBLANES) * V7X_SUBLANES
    cond_pad = jnp.pad(cond, ((0, cond_rows - cond.shape[0]), (0, 0)))
    tile_row0 = jnp.arange(n // ROW_TILE) * ROW_TILE
    tile_cond = jnp.where(tile_row0 < n_ctx, 0, 1 + (tile_row0 - n_ctx) // dec_seq)

    ctx_states = []
    for l in range(depth):
        w = w_in[l]
        w_in_r = jnp.concatenate(
            [w[:, :main], w[:, main + 2 * gk_rank:], w[:, main:main + 2 * gk_rank],
             jnp.zeros((d, V7X_LANES - 2 * gk_rank), w.dtype)], axis=1).astype(BF16)
        wup_pad = jnp.zeros((2, V7X_LANES, d_k), F32)
        wup_pad = wup_pad.at[0, :gk_rank].set(w_gk_up[l, 0]).at[1, gk_rank:2 * gk_rank].set(w_gk_up[l, 1])
        wup_pad = wup_pad.astype(BF16)
        bgk = b_gk[l].reshape(2, 1, d_k)

        mod = _ada_mod(cond_pad, w_ada[l], b_ada[l]).reshape(cond_rows, N_ADA, d)
        mod_t = jnp.pad(mod[tile_cond], ((0, 0), (0, V7X_SUBLANES - N_ADA), (0, 0)))

        z = _in_proj(x, mod_t, w_in_r)

        p = jnp.concatenate([
            _conv_mixer(z, w_conv[l], row0=0, n_seq=n_b, seq_len=seq, row_len=seq, latent=False),
            _conv_mixer(z, w_conv[l], row0=n_ctx, n_seq=dec_b, seq_len=dec_seq, row_len=GRID_W,
                        latent=True)], axis=0)

        of_c, ob_c, s_fin = _gla(z, wup_pad, bgk, None, row0=0, n_seq=n_b, seq_len=seq,
                                 d_k=d_k, d_v=d_v, cols=cols)
        s0_t = jnp.swapaxes(state_gla[:, l], -1, -2)
        of_l, ob_l = _gla(z, wup_pad, bgk, s0_t, row0=n_ctx, n_seq=dec_b, seq_len=dec_seq,
                          d_k=d_k, d_v=d_v, cols=cols)
        ctx_states.append(jnp.swapaxes(s_fin, -1, -2))
        og = jnp.concatenate([
            _gla_post(of_c, ob_c, z, w_gla_norm[l], row0=0, r_col=cols["r"]),
            _gla_post(of_l, ob_l, z, w_gla_norm[l], row0=n_ctx, r_col=cols["r"])], axis=0)

        y = _merge(p, og, z, w_a[l].astype(BF16), w_b[l].astype(BF16),
                   ga_col=cols["ga"], gb_col=cols["gb"])
        x1, ht = _out_proj(y, x, mod_t, w_o[l].astype(BF16), ln_g[l, 0], ln_b[l, 0], alpha=alpha)

        a1, r, a2 = _router(ht, w_pq[l].T.astype(BF16), peer_keys[l].astype(BF16))
        ft = _experts(ht, a1, r, a2, peer_u[l].astype(BF16), peer_v[l].T.astype(BF16))
        x = _ffn_out(ft, x1, mod_t, ln_g[l, 1], ln_b[l, 1], alpha=alpha)

    new_state = jnp.stack(ctx_states, axis=1).astype(x_prompt.dtype)
    return (x[:n_ctx].reshape(n_b, seq, d), x[n_ctx:].reshape(dec_b, dec_seq, d), new_state)
```

```python
import functools

import jax
import jax.numpy as jnp
from jax import lax
from jax.experimental import pallas as pl
from jax.experimental.pallas import tpu as pltpu

F32 = jnp.float32
BF16 = jnp.bfloat16

LN_EPS = 1e-5
GATE_NORM = 16.0
LOG_DECAY_MIN = -1.0
GLA_CHUNK = 64
GLA_HEADS = 4
GRID_W = 64
PEER_HEADS = 8
PEER_TOPK = 16
N_ADA = 6

V7X_LANES = 128
V7X_SUBLANES = 8
V7X_VMEM_LIMIT_BYTES = 56 * 1024 * 1024

ROW_TILE = 512
IN_PROJ_COL_TILE = 1920
MERGE_COL_TILE = 1024
CONV_CH_TILE = 256
GLA_STEP_ROWS = 256
ROUTER_TOK_TILE = 256
PEER_TOK_TILE = 512
PEER_EXPERT_TILE = 1024
ADA_COL_TILE = 1024


def _params(n_grid_dims):
    return pltpu.CompilerParams(
        dimension_semantics=("arbitrary",) * n_grid_dims,
        vmem_limit_bytes=V7X_VMEM_LIMIT_BYTES)


def _sigmoid(x):
    return 1.0 / (1.0 + jnp.exp(-x))


def _gelu_tanh(x):
    return 0.5 * x * (1.0 + jnp.tanh(0.7978845608028654 * (x + 0.044715 * (x * x * x))))


def _log_sigmoid(x):
    return jnp.minimum(x, 0.0) - jnp.log(1.0 + jnp.exp(-jnp.abs(x)))


def _ln(x):
    mu = jnp.mean(x, axis=-1, keepdims=True)
    xc = x - mu
    var = jnp.mean(xc * xc, axis=-1, keepdims=True)
    return xc * lax.rsqrt(var + LN_EPS)


def _dot(a, b):
    return jnp.dot(a, b, preferred_element_type=F32)


def _dot_nt(a, b):
    return lax.dot_general(a, b, (((1,), (1,)), ((), ())), preferred_element_type=F32)


def _dot_tn(a, b):
    return lax.dot_general(a, b, (((0,), (0,)), ((), ())), preferred_element_type=F32)


def _pack_rows(w):
    r2, c = w.shape
    return lax.bitcast_convert_type(jnp.swapaxes(w.reshape(r2 // 2, 2, c), 1, 2), jnp.uint32)


def _ada_body(c_ref, w_ref, b_ref, o_ref):
    c = c_ref[...]
    s = (c * _sigmoid(c)).astype(BF16)
    o_ref[...] = _dot(s, w_ref[...].astype(BF16)) + b_ref[...]


def _ada_mod(cond_pad, w_ada, b_ada):
    rows, d = cond_pad.shape
    n_out = w_ada.shape[1]
    tn = ADA_COL_TILE
    return pl.pallas_call(
        _ada_body,
        grid=(n_out // tn,),
        in_specs=[pl.BlockSpec((rows, d), lambda j: (0, 0)),
                  pl.BlockSpec((d, tn), lambda j: (0, j)),
                  pl.BlockSpec((1, tn), lambda j: (0, j))],
        out_specs=pl.BlockSpec((rows, tn), lambda j: (0, j)),
        out_shape=jax.ShapeDtypeStruct((rows, n_out), F32),
        compiler_params=_params(1),
        name="ada_mod",
    )(cond_pad, w_ada, b_ada.reshape(1, n_out))


def _segment_specs(segments, tm, d):
    specs, bounds, start = [], [], 0
    for seg in segments:
        tiles = seg.shape[0] // tm
        specs.append(pl.BlockSpec(
            (tm, d), lambda i, *_, s=start, t=tiles: (jnp.clip(i - s, 0, t - 1), 0)))
        bounds.append((start, start + tiles))
        start += tiles
    return specs, bounds


def _in_proj_body(*refs, bounds):
    x_refs = refs[:len(bounds)]
    mod_ref, w_ref, z_ref, h_scr = refs[len(bounds):]
    i = pl.program_id(0)
    for x_ref, (lo, hi) in zip(x_refs, bounds):
        @pl.when((pl.program_id(1) == 0) & (i >= lo) & (i < hi))
        def _():
            m = mod_ref[0]
            h = _ln(x_ref[...]) * (1.0 + m[1:2]) + m[0:1]
            h_scr[...] = h.astype(BF16)

    z_ref[...] = _dot(h_scr[...], w_ref[...])


def _in_proj(x_segments, mod_t, w_in_r):
    d = x_segments[0].shape[1]
    n = sum(seg.shape[0] for seg in x_segments)
    nc = w_in_r.shape[1]
    tm, tn = ROW_TILE, IN_PROJ_COL_TILE
    x_specs, bounds = _segment_specs(x_segments, tm, d)
    return pl.pallas_call(
        functools.partial(_in_proj_body, bounds=bounds),
        grid=(n // tm, nc // tn),
        in_specs=x_specs + [pl.BlockSpec((1, V7X_SUBLANES, d), lambda i, j: (i, 0, 0)),
                            pl.BlockSpec((d, tn), lambda i, j: (0, j))],
        out_specs=pl.BlockSpec((tm, tn), lambda i, j: (i, j)),
        out_shape=jax.ShapeDtypeStruct((n, nc), F32),
        scratch_shapes=[pltpu.VMEM((tm, d), BF16)],
        compiler_params=_params(2),
        name="in_proj",
    )(*x_segments, mod_t, w_in_r)


def _conv3_rows(u, w, shift, period):
    n = u.shape[0]
    pos = lax.broadcasted_iota(jnp.int32, u.shape, 0) % period
    prev = jnp.where(pos >= shift, pltpu.roll(u, shift, axis=0), 0.0)
    nxt = jnp.where(pos < period - shift, pltpu.roll(u, n - shift, axis=0), 0.0)
    return prev * w[0:1] + u * w[1:2] + nxt * w[2:3]


def _conv_body(cb_ref, cc_ref, cx_ref, w_ref, p_ref, *, n_ctx_units, ctx_seq, n_horizontal_tiles):
    u = cc_ref[...] * cx_ref[...]
    w = w_ref[...]
    n = u.shape[0]
    is_ctx = pl.program_id(0) < n_ctx_units
    is_horizontal = pl.program_id(1) < n_horizontal_tiles

    @pl.when(is_ctx)
    def _():
        p_ref[...] = (cb_ref[...] * _conv3_rows(u, w, 1, ctx_seq)).astype(BF16)

    @pl.when(jnp.logical_not(is_ctx) & is_horizontal)
    def _():
        p_ref[...] = (cb_ref[...] * _conv3_rows(u, w, 1, GRID_W)).astype(BF16)

    @pl.when(jnp.logical_not(is_ctx) & jnp.logical_not(is_horizontal))
    def _():
        p_ref[...] = (cb_ref[...] * _conv3_rows(u, w, GRID_W, n)).astype(BF16)


def _conv_mixer(z, w_conv, *, n_ctx, ctx_seq, dec_seq):
    n = z.shape[0]
    d_conv = w_conv.shape[1]
    ct = CONV_CH_TILE
    n_ct = d_conv // ct
    body = functools.partial(_conv_body, n_ctx_units=n_ctx // dec_seq, ctx_seq=ctx_seq,
                             n_horizontal_tiles=n_ct // 2)
    return pl.pallas_call(
        body,
        grid=(n // dec_seq, n_ct),
        in_specs=[pl.BlockSpec((dec_seq, ct), lambda b, c: (b, c)),
                  pl.BlockSpec((dec_seq, ct), lambda b, c: (b, n_ct + c)),
                  pl.BlockSpec((dec_seq, ct), lambda b, c: (b, 2 * n_ct + c)),
                  pl.BlockSpec((3, ct), lambda b, c: (0, c))],
        out_specs=pl.BlockSpec((dec_seq, ct), lambda b, c: (b, c)),
        out_shape=jax.ShapeDtypeStruct((n, d_conv), BF16),
        compiler_params=_params(2),
        name="conv_mixer",
    )(z, z, z, w_conv)


def _split2_dot(tri, g):
    g1 = g.astype(BF16)
    g2 = (g - g1.astype(F32)).astype(BF16)
    return _dot(tri, g1) + _dot(tri, g2)


def _gla_direction(q_ref, k_ref, v_ref, l_ref, wup, bgk, o_ref, s_scr, *, backward, q_scale):
    rows = q_ref.shape[0]
    n_chunks = rows // GLA_CHUNK
    shift = GLA_CHUNK.bit_length() - 1
    t_idx = lax.broadcasted_iota(jnp.int32, (rows, rows), 0)
    s_idx = lax.broadcasted_iota(jnp.int32, (rows, rows), 1)
    t_chunk = jnp.right_shift(t_idx, shift)
    s_chunk = jnp.right_shift(s_idx, shift)
    lag = (s_chunk - t_chunk) if backward else (t_chunk - s_chunk)
    causal = (s_idx >= t_idx) if backward else (s_idx <= t_idx)
    tri = jnp.where(lag == 0, jnp.where(causal, 1.0, 0.0), 0.0).astype(BF16)

    gpre = _dot(l_ref[...].astype(BF16), wup) + bgk
    g = jnp.maximum(_log_sigmoid(gpre) * (1.0 / GATE_NORM), LOG_DECAY_MIN)
    yield
    b = _split2_dot(tri, g)
    yield

    order = list(range(n_chunks - 1, -1, -1)) if backward else list(range(n_chunks))
    scan_pos = {c: j for j, c in enumerate(order)}

    def chunk_total(c):
        row = c * GLA_CHUNK if backward else (c + 1) * GLA_CHUNK - 1
        return b[row:row + 1]

    totals = [chunk_total(c) for c in order]
    prefix = [jnp.zeros_like(totals[0])]
    for t in totals[:-1]:
        prefix.append(prefix[-1] + t)
    total = prefix[-1] + totals[-1]

    def per_chunk_rows(fn):
        return jnp.concatenate(
            [jnp.broadcast_to(fn(scan_pos[c]), (GLA_CHUNK, b.shape[1])) for c in range(n_chunks)],
            axis=0)

    q = q_ref[...] * q_scale
    k = k_ref[...]
    v = v_ref[...].astype(BF16)
    qe = q * jnp.exp(b)
    ke = (k * jnp.exp(-b)).astype(BF16)
    kd = k * jnp.exp(per_chunk_rows(lambda j: totals[j]) - b)
    qe_b = qe.astype(BF16)
    kd_b = kd.astype(BF16)
    yield

    att = jnp.where(lag == 0, jnp.where(causal, _dot_nt(qe_b, ke), 0.0), 0.0)
    for dist in range(1, n_chunks):
        if dist == 1:
            q_dist = qe_b
        else:
            between = per_chunk_rows(
                lambda j: jnp.exp(prefix[j] - prefix[j - dist + 1]) if j >= dist
                else jnp.ones_like(total))
            q_dist = (qe * between).astype(BF16)
        att = jnp.where(lag == dist, _dot_nt(q_dist, kd_b), att)

    q_in = (qe * per_chunk_rows(lambda j: jnp.exp(prefix[j]))).astype(BF16)
    k_out = (kd * per_chunk_rows(lambda j: jnp.exp(total - prefix[j] - totals[j]))).astype(BF16)
    yield
    s = s_scr[...]
    o_ref[...] = _dot(att.astype(BF16), v) + _dot_nt(q_in, s.astype(BF16))
    s_scr[...] = s * jnp.exp(total) + _dot_tn(v, k_out)
    yield


def _gla_body(*refs, zero_init, q_scale):
    if zero_init:
        (qf, kf, vf, lf, qb, kb, vb, lb, wup_ref, bgk_ref,
         of_ref, ob_ref, sfin_ref, sf_scr, sb_scr) = refs
    else:
        (qf, kf, vf, lf, qb, kb, vb, lb, wup_ref, bgk_ref, s0_ref,
         of_ref, ob_ref, sf_scr, sb_scr) = refs
    step = pl.program_id(2)

    @pl.when(step == 0)
    def _():
        if zero_init:
            sf_scr[...] = jnp.zeros_like(sf_scr)
            sb_scr[...] = jnp.zeros_like(sb_scr)
        else:
            sf_scr[...] = s0_ref[0, 0, 0]
            sb_scr[...] = s0_ref[0, 1, 0]

    directions = [
        _gla_direction(qf, kf, vf, lf, wup_ref[0], bgk_ref[0], of_ref, sf_scr,
                       backward=False, q_scale=q_scale),
        _gla_direction(qb, kb, vb, lb, wup_ref[1], bgk_ref[1], ob_ref, sb_scr,
                       backward=True, q_scale=q_scale)]
    for _ in zip(*directions):
        pass

    if zero_init:
        @pl.when(step == pl.num_programs(2) - 1)
        def _():
            sfin_ref[0, 0, 0] = sf_scr[...]
            sfin_ref[0, 1, 0] = sb_scr[...]


def _gla(z, wup_pad, bgk, s0_t, *, row0, n_seq, seq_len, d_k, d_v, cols):
    dk, dv = d_k // GLA_HEADS, d_v // GLA_HEADS
    rs = min(GLA_STEP_ROWS, seq_len)
    steps = seq_len // rs
    blk0 = row0 // rs
    q_blk, k_blk, v_blk, l_blk = (cols["q"] // dk, cols["k"] // dk, cols["v"] // dv,
                                  cols["lfb"] // V7X_LANES)
    zero_init = s0_t is None

    def fwd(col_blk):
        return lambda b, h, s: (blk0 + b * steps + s, col_blk(h))

    def bwd(col_blk):
        return lambda b, h, s: (blk0 + b * steps + (steps - 1 - s), col_blk(h))

    def dir_specs(mk):
        return [pl.BlockSpec((rs, dk), mk(lambda h: q_blk + h)),
                pl.BlockSpec((rs, dk), mk(lambda h: k_blk + h)),
                pl.BlockSpec((rs, dv), mk(lambda h: v_blk + h)),
                pl.BlockSpec((rs, V7X_LANES), mk(lambda h: l_blk))]

    in_specs = dir_specs(fwd) + dir_specs(bwd) + [
        pl.BlockSpec((2, V7X_LANES, dk), lambda b, h, s: (0, 0, h)),
        pl.BlockSpec((2, 1, dk), lambda b, h, s: (0, 0, h))]
    args = [z] * 8 + [wup_pad, bgk]
    n_rows = n_seq * seq_len
    out_shape = [jax.ShapeDtypeStruct((n_rows, d_v), F32)] * 2
    out_specs = [pl.BlockSpec((rs, dv), lambda b, h, s: (b * steps + s, h)),
                 pl.BlockSpec((rs, dv), lambda b, h, s: (b * steps + (steps - 1 - s), h))]
    state_spec = pl.BlockSpec((1, 2, 1, dv, dk), lambda b, h, s: (b, 0, h, 0, 0))
    if zero_init:
        out_shape.append(jax.ShapeDtypeStruct((n_seq, 2, GLA_HEADS, dv, dk), F32))
        out_specs.append(state_spec)
    else:
        in_specs.append(state_spec)
        args.append(s0_t)
    body = functools.partial(_gla_body, zero_init=zero_init, q_scale=float(dk) ** -0.5)
    return pl.pallas_call(
        body,
        grid=(n_seq, GLA_HEADS, steps),
        in_specs=in_specs,
        out_specs=out_specs,
        out_shape=out_shape,
        scratch_shapes=[pltpu.VMEM((dv, dk), F32), pltpu.VMEM((dv, dk), F32)],
        compiler_params=_params(3),
        name="gla_context" if zero_init else "gla_latent",
    )(*args)


def _gla_post_body(*refs, bounds):
    n_seg = len(bounds)
    of_refs, ob_refs = refs[:n_seg], refs[n_seg:2 * n_seg]
    r_ref, w_ref, o_ref = refs[2 * n_seg:]
    i = pl.program_id(0)
    for of_ref, ob_ref, (lo, hi) in zip(of_refs, ob_refs, bounds):
        @pl.when((i >= lo) & (i < hi))
        def _():
            o = of_ref[...] + ob_ref[...]
            o = o * lax.rsqrt(jnp.mean(o * o, axis=-1, keepdims=True) + LN_EPS) * w_ref[...]
            r = r_ref[...]
            o_ref[...] = (o * (r * _sigmoid(r))).astype(BF16)


def _gla_post(o_f_segments, o_b_segments, z, w_norm, *, r_col):
    d_v = o_f_segments[0].shape[1]
    n = sum(seg.shape[0] for seg in o_f_segments)
    dv = d_v // GLA_HEADS
    tm = ROW_TILE
    r_blk = r_col // dv
    bounds, seg_specs, start = [], [], 0
    for seg in o_f_segments:
        tiles = seg.shape[0] // tm
        seg_specs.append(pl.BlockSpec(
            (tm, dv), lambda i, h, s=start, t=tiles: (
                jnp.clip(i - s, 0, t - 1), jnp.where((i >= s) & (i < s + t), h, 0))))
        bounds.append((start, start + tiles))
        start += tiles
    return pl.pallas_call(
        functools.partial(_gla_post_body, bounds=bounds),
        grid=(n // tm, GLA_HEADS),
        in_specs=seg_specs + seg_specs + [
            pl.BlockSpec((tm, dv), lambda i, h: (i, r_blk + h)),
            pl.BlockSpec((1, dv), lambda i, h: (0, h))],
        out_specs=pl.BlockSpec((tm, dv), lambda i, h: (i, h)),
        out_shape=jax.ShapeDtypeStruct((n, d_v), BF16),
        compiler_params=_params(2),
        name="gla_post",
    )(*o_f_segments, *o_b_segments, z, w_norm.reshape(1, d_v))


def _merge_body(p_ref, og_ref, ga_ref, gb_ref, wa_ref, wb_ref, y_ref):
    y_a = _dot(p_ref[...], wa_ref[...])
    y_b = _dot(og_ref[...], wb_ref[...])
    y_ref[...] = (_sigmoid(ga_ref[...]) * y_a + _sigmoid(gb_ref[...]) * y_b).astype(BF16)


def _merge(p, og, z, w_a, w_b, *, ga_col, gb_col):
    n, d_conv = p.shape
    d_v = og.shape[1]
    d = w_a.shape[1]
    tm, tn = ROW_TILE, MERGE_COL_TILE
    ga_blk, gb_blk = ga_col // tn, gb_col // tn
    return pl.pallas_call(
        _merge_body,
        grid=(n // tm, d // tn),
        in_specs=[pl.BlockSpec((tm, d_conv), lambda i, j: (i, 0)),
                  pl.BlockSpec((tm, d_v), lambda i, j: (i, 0)),
                  pl.BlockSpec((tm, tn), lambda i, j: (i, ga_blk + j)),
                  pl.BlockSpec((tm, tn), lambda i, j: (i, gb_blk + j)),
                  pl.BlockSpec((d_conv, tn), lambda i, j: (0, j)),
                  pl.BlockSpec((d_v, tn), lambda i, j: (0, j))],
        out_specs=pl.BlockSpec((tm, tn), lambda i, j: (i, j)),
        out_shape=jax.ShapeDtypeStruct((n, d), BF16),
        compiler_params=_params(2),
        name="merge",
    )(p, og, z, z, w_a, w_b)


def _out_proj_body(*refs, alpha, bounds):
    x_refs = refs[:len(bounds)]
    y_ref, mod_ref, wo_ref, lg_ref, lb_ref, x1_ref, ht_ref = refs[len(bounds):]
    m = mod_ref[0]
    mix = m[2:3] * _dot(y_ref[...], wo_ref[...])
    i = pl.program_id(0)
    for x_ref, (lo, hi) in zip(x_refs, bounds):
        @pl.when((i >= lo) & (i < hi))
        def _():
            x1 = _ln(alpha * x_ref[...] + mix) * lg_ref[...] + lb_ref[...]
            x1_ref[...] = x1
            h = _ln(x1) * (1.0 + m[4:5]) + m[3:4]
            ht_ref[...] = h.T.astype(BF16)


def _out_proj(y, x_segments, mod_t, w_o, ln_g, ln_b, *, alpha):
    n, d = y.shape
    tm = ROW_TILE
    x_specs, bounds = _segment_specs(x_segments, tm, d)
    return pl.pallas_call(
        functools.partial(_out_proj_body, alpha=alpha, bounds=bounds),
        grid=(n // tm,),
        in_specs=x_specs + [
                  pl.BlockSpec((tm, d), lambda i: (i, 0)),
                  pl.BlockSpec((1, V7X_SUBLANES, d), lambda i: (i, 0, 0)),
                  pl.BlockSpec((d, d), lambda i: (0, 0)),
                  pl.BlockSpec((1, d), lambda i: (0, 0)),
                  pl.BlockSpec((1, d), lambda i: (0, 0))],
        out_specs=[pl.BlockSpec((tm, d), lambda i: (i, 0)),
                   pl.BlockSpec((d, tm), lambda i: (0, i))],
        out_shape=[jax.ShapeDtypeStruct((n, d), F32),
                   jax.ShapeDtypeStruct((d, n), BF16)],
        compiler_params=_params(1),
        name="out_proj",
    )(*x_segments, y, mod_t, w_o, ln_g.reshape(1, d), ln_b.reshape(1, d))


def _sorting_network(n):
    pairs, p = [], 1
    while p < n:
        k = p
        while k >= 1:
            for j in range(k % p, n - k, 2 * k):
                for i in range(min(k, n - j - k)):
                    if (i + j) // (2 * p) == (i + j + k) // (2 * p):
                        pairs.append((i + j, i + j + k))
            k //= 2
        p *= 2
    return pairs


def _bitonic_merge_network(n):
    pairs, k = [], n // 2
    while k >= 1:
        pairs += [(i, i + k) for i in range(n) if (i // k) % 2 == 0]
        k //= 2
    return pairs


def _compare_exchange(x, pairs):
    for i, j in pairs:
        x[i], x[j] = jnp.maximum(x[i], x[j]), jnp.minimum(x[i], x[j])
    return x


def _top_sorted(s):
    sub = V7X_SUBLANES
    m = s.shape[0] // sub
    out = []
    for lane0 in range(0, s.shape[1], V7X_LANES):
        lanes = slice(lane0, lane0 + V7X_LANES)
        x = _compare_exchange([s[sub * k:sub * (k + 1), lanes] for k in range(m)],
                              _sorting_network(m))
        shift = sub // 2
        while shift >= 1:
            partner = [pltpu.roll(v, shift, axis=0) for v in x]
            x = _compare_exchange([jnp.maximum(x[k], partner[m - 1 - k]) for k in range(m)],
                                  _bitonic_merge_network(m))
            shift //= 2
        out.append(jnp.concatenate([v[0:1] for v in x], axis=0))
    return jnp.concatenate(out, axis=1)


def _router_body(ht_ref, wpq_ref, keys_ref, a1_ref, r_ref, a2_ref, *, half_dim):
    ht = ht_ref[...]
    k1 = keys_ref[0]
    k2 = keys_ref[1]
    k = PEER_TOPK
    sub = V7X_SUBLANES

    def candidates(v1, v2):
        groups = [v1[0:1] * v2[0:sub], v1[0:1] * v2[sub:k]]
        groups += [v1[a:a + 1] * v2[0:sub] for a in range(1, sub)]
        groups.append(v1[sub:k] * v2[0:1])
        return jnp.concatenate(groups, axis=0)

    for h in range(PEER_HEADS):
        qq = _dot(pltpu.bitcast(wpq_ref[h * half_dim:(h + 1) * half_dim, :], BF16), ht)
        q1 = qq[:half_dim].astype(BF16)
        q2 = qq[half_dim:].astype(BF16)
        s1 = _dot(k1, q1)
        s2 = _dot(k2, q2)
        t1 = _top_sorted(s1)
        t2 = _top_sorted(s2)
        a1 = jnp.exp(s1 - t1[0:1])
        a2 = jnp.exp(s2 - t2[0:1])
        v1 = jnp.exp(t1 - t1[0:1])
        v2 = jnp.exp(t2 - t2[0:1])
        cand = candidates(v1, v2)
        c = cand
        for _ in range(k - 1):
            m = jnp.max(c, axis=0, keepdims=True)
            c = jnp.where(c == m, -1.0, c)
        tau = jnp.maximum(jnp.max(c, axis=0, keepdims=True), 0.0)
        sel = cand >= tau
        inv_z = 1.0 / jnp.sum(jnp.where(sel, cand, 0.0), axis=0, keepdims=True)
        theta = jnp.min(jnp.where(sel, candidates(v1, v2 * inv_z), jnp.inf), axis=0, keepdims=True)
        a1_ref[h] = a1
        r_ref[h] = (theta * (1.0 - 2.0 ** -20)) / a1
        a2_ref[h] = (a2 * inv_z).astype(BF16)


def _router(ht, wpq_t, keys):
    d, n = ht.shape
    n_keys, half_dim = keys.shape[1], keys.shape[2]
    assert n_keys == PEER_TOPK * V7X_SUBLANES
    tt = ROUTER_TOK_TILE
    blk = pl.BlockSpec((PEER_HEADS, n_keys, tt), lambda i: (0, 0, i))
    return pl.pallas_call(
        functools.partial(_router_body, half_dim=half_dim),
        grid=(n // tt,),
        in_specs=[pl.BlockSpec((d, tt), lambda i: (0, i)),
                  pl.BlockSpec(wpq_t.shape, lambda i: (0, 0)),
                  pl.BlockSpec(keys.shape, lambda i: (0, 0, 0))],
        out_specs=[blk, blk, blk],
        out_shape=[jax.ShapeDtypeStruct((PEER_HEADS, n_keys, n), F32),
                   jax.ShapeDtypeStruct((PEER_HEADS, n_keys, n), F32),
                   jax.ShapeDtypeStruct((PEER_HEADS, n_keys, n), BF16)],
        compiler_params=_params(1),
        name="peer_router",
    )(ht, wpq_t, keys)


def _experts_body(ht_ref, a1_ref, r_ref, a2_ref, wu_ref, wvt_ref, ft_ref, gc_scr, *, n_keys):
    @pl.when(pl.program_id(1) == 0)
    def _():
        ft_ref[...] = jnp.zeros_like(ft_ref)

    zero = jnp.zeros((), BF16)
    u = _dot(pltpu.bitcast(wu_ref[...], BF16), ht_ref[...])
    for ii in range(u.shape[0] // n_keys):
        rows = slice(ii * n_keys, (ii + 1) * n_keys)
        coef = None
        for h in range(PEER_HEADS):
            a2 = a2_ref[h]
            bound = r_ref[h, ii:ii + 1, :].astype(BF16)
            weight = a1_ref[h, ii:ii + 1, :].astype(BF16)
            term = jnp.where(a2 >= bound, a2, zero) * weight
            coef = term if coef is None else coef + term
        gc_scr[rows, :] = coef * _gelu_tanh(u[rows]).astype(BF16)
    ft_ref[...] += _dot(pltpu.bitcast(wvt_ref[...], BF16), gc_scr[...])


def _experts(ht, a1, r, a2, w_u, w_vt):
    d, n = ht.shape
    n_exp = 2 * w_u.shape[0]
    n_keys = a1.shape[1]
    tt, te = PEER_TOK_TILE, PEER_EXPERT_TILE
    rows_per_step = te // n_keys
    row_blk = pl.BlockSpec((PEER_HEADS, rows_per_step, tt), lambda i, e: (0, e, i))
    return pl.pallas_call(
        functools.partial(_experts_body, n_keys=n_keys),
        grid=(n // tt, n_exp // te),
        in_specs=[pl.BlockSpec((d, tt), lambda i, e: (0, i)),
                  row_blk, row_blk,
                  pl.BlockSpec((PEER_HEADS, n_keys, tt), lambda i, e: (0, 0, i)),
                  pl.BlockSpec((te // 2, d), lambda i, e: (e, 0)),
                  pl.BlockSpec((d // 2, te), lambda i, e: (0, e))],
        out_specs=pl.BlockSpec((d, tt), lambda i, e: (0, i)),
        out_shape=jax.ShapeDtypeStruct((d, n), F32),
        scratch_shapes=[pltpu.VMEM((te, tt), BF16)],
        compiler_params=_params(2),
        name="peer_experts",
    )(ht, a1, r, a2, w_u, w_vt)


def _ffn_out_body(ft_ref, x1_ref, mod_ref, lg_ref, lb_ref, *o_refs, alpha, bounds):
    m = mod_ref[0]
    f = ft_ref[...].T
    out = _ln(alpha * x1_ref[...] + m[5:6] * f) * lg_ref[...] + lb_ref[...]
    i = pl.program_id(0)
    for o_ref, (lo, hi) in zip(o_refs, bounds):
        @pl.when((i >= lo) & (i < hi))
        def _():
            o_ref[...] = out


def _ffn_out(ft, x1, mod_t, ln_g, ln_b, *, alpha, out_rows):
    n, d = x1.shape
    tm = ROW_TILE
    out_specs, out_shape, bounds, start = [], [], [], 0
    for rows in out_rows:
        tiles = rows // tm
        out_specs.append(pl.BlockSpec(
            (tm, d), lambda i, s=start, t=tiles: (jnp.clip(i - s, 0, t - 1), 0)))
        out_shape.append(jax.ShapeDtypeStruct((rows, d), F32))
        bounds.append((start, start + tiles))
        start += tiles
    return pl.pallas_call(
        functools.partial(_ffn_out_body, alpha=alpha, bounds=bounds),
        grid=(n // tm,),
        in_specs=[pl.BlockSpec((d, tm), lambda i: (0, i)),
                  pl.BlockSpec((tm, d), lambda i: (i, 0)),
                  pl.BlockSpec((1, V7X_SUBLANES, d), lambda i: (i, 0, 0)),
                  pl.BlockSpec((1, d), lambda i: (0, 0)),
                  pl.BlockSpec((1, d), lambda i: (0, 0))],
        out_specs=out_specs,
        out_shape=out_shape,
        compiler_params=_params(1),
        name="ffn_out",
    )(ft, x1, mod_t, ln_g.reshape(1, d), ln_b.reshape(1, d))


def kernel(x_prompt, x_sample, state_gla, c, c_ctx, w_in, w_conv, w_a, w_gk_up, b_gk, w_gla_norm,
           w_b, w_o, w_ada, b_ada, ln_g, ln_b, w_pq, peer_keys, peer_u, peer_v):
    n_b, seq, d = x_prompt.shape
    dec_b, dec_seq, _ = x_sample.shape
    depth = w_in.shape[0]
    d_conv = w_conv.shape[2]
    d_k = b_gk.shape[2]
    d_v = w_b.shape[1]
    gk_rank = w_gk_up.shape[2]
    n_ctx, n_lat = n_b * seq, dec_b * dec_seq
    n = n_ctx + n_lat
    alpha = (2.0 * depth) ** 0.25
    assert n_ctx % ROW_TILE == 0 and dec_seq % ROW_TILE == 0 and n_ctx % dec_seq == 0
    assert seq % GLA_CHUNK == 0 and dec_seq % GLA_STEP_ROWS == 0 and dec_seq % GRID_W == 0
    assert dec_seq % seq == 0 and 2 * gk_rank <= V7X_LANES

    main = 3 * d_conv + 2 * d_k + 2 * d_v
    cols = {"cb": 0, "q": 3 * d_conv, "k": 3 * d_conv + d_k, "v": 3 * d_conv + 2 * d_k,
            "r": 3 * d_conv + 2 * d_k + d_v, "ga": main, "gb": main + d, "lfb": main + 2 * d}
    n_cols = cols["lfb"] + V7X_LANES
    assert n_cols % IN_PROJ_COL_TILE == 0

    x_segments = [x_prompt.reshape(n_ctx, d), x_sample.reshape(n_lat, d)]
    cond = jnp.concatenate([c_ctx[None, :], c], axis=0)
    cond_rows = -(-cond.shape[0] // V7X_SUBLANES) * V7X_SUBLANES
    cond_pad = jnp.pad(cond, ((0, cond_rows - cond.shape[0]), (0, 0)))
    tile_row0 = jnp.arange(n // ROW_TILE) * ROW_TILE
    tile_cond = jnp.where(tile_row0 < n_ctx, 0, 1 + (tile_row0 - n_ctx) // dec_seq)

    ctx_states = []
    for l in range(depth):
        w = w_in[l]
        w_in_r = jnp.concatenate(
            [w[:, :main], w[:, main + 2 * gk_rank:], w[:, main:main + 2 * gk_rank],
             jnp.zeros((d, V7X_LANES - 2 * gk_rank), w.dtype)], axis=1).astype(BF16)
        wup_pad = jnp.zeros((2, V7X_LANES, d_k), F32)
        wup_pad = wup_pad.at[0, :gk_rank].set(w_gk_up[l, 0]).at[1, gk_rank:2 * gk_rank].set(w_gk_up[l, 1])
        wup_pad = wup_pad.astype(BF16)
        bgk = b_gk[l].reshape(2, 1, d_k)

        mod = _ada_mod(cond_pad, w_ada[l], b_ada[l]).reshape(cond_rows, N_ADA, d)
        mod_t = jnp.pad(mod[tile_cond], ((0, 0), (0, V7X_SUBLANES - N_ADA), (0, 0)))

        z = _in_proj(x_segments, mod_t, w_in_r)
        p = _conv_mixer(z, w_conv[l], n_ctx=n_ctx, ctx_seq=seq, dec_seq=dec_seq)

        of_c, ob_c, s_fin = _gla(z, wup_pad, bgk, None, row0=0, n_seq=n_b, seq_len=seq,
                                 d_k=d_k, d_v=d_v, cols=cols)
        s0_t = jnp.swapaxes(state_gla[:, l], -1, -2)
        of_l, ob_l = _gla(z, wup_pad, bgk, s0_t, row0=n_ctx, n_seq=dec_b, seq_len=dec_seq,
                          d_k=d_k, d_v=d_v, cols=cols)
        ctx_states.append(jnp.swapaxes(s_fin, -1, -2))
        og = _gla_post([of_c, of_l], [ob_c, ob_l], z, w_gla_norm[l], r_col=cols["r"])

        y = _merge(p, og, z, w_a[l].astype(BF16), w_b[l].astype(BF16),
                   ga_col=cols["ga"], gb_col=cols["gb"])
        x1, ht = _out_proj(y, x_segments, mod_t, w_o[l].astype(BF16), ln_g[l, 0], ln_b[l, 0],
                           alpha=alpha)

        a1, r, a2 = _router(ht, _pack_rows(w_pq[l].T.astype(BF16)), peer_keys[l].astype(BF16))
        ft = _experts(ht, a1, r, a2, _pack_rows(peer_u[l].astype(BF16)),
                      _pack_rows(peer_v[l].T.astype(BF16)))
        last = l == depth - 1
        x_segments = _ffn_out(ft, x1, mod_t, ln_g[l, 1], ln_b[l, 1], alpha=alpha,
                              out_rows=[n_ctx, n_lat] if last else [n])

    new_state = jnp.stack(ctx_states, axis=1).astype(x_prompt.dtype)
    xp, xs = x_segments
    return (xp.reshape(n_b, seq, d), xs.reshape(dec_b, dec_seq, d), new_state)
```

```python
import functools

import jax
import jax.numpy as jnp
from jax import lax
from jax.experimental import pallas as pl
from jax.experimental.pallas import tpu as pltpu

F32 = jnp.float32
BF16 = jnp.bfloat16

LN_EPS = 1e-5
GATE_NORM = 16.0
LOG_DECAY_MIN = -1.0
GLA_CHUNK = 64
GLA_HEADS = 4
GRID_W = 64
PEER_HEADS = 8
PEER_TOPK = 16
N_ADA = 6

V7X_LANES = 128
V7X_SUBLANES = 8
V7X_VMEM_LIMIT_BYTES = 60 * 1024 * 1024

ROW_TILE = 512
IN_PROJ_COL_TILE = 1920
MERGE_COL_TILE = 1024
CONV_CH_TILE = 256
GLA_STEP_ROWS = 256
ROUTER_TOK_TILE = 512
PEER_TOK_TILE = 512
PEER_EXPERT_TILE = 2048
ADA_COL_TILE = 1024


def _params(n_grid_dims):
    return pltpu.CompilerParams(
        dimension_semantics=("arbitrary",) * n_grid_dims,
        vmem_limit_bytes=V7X_VMEM_LIMIT_BYTES)


def _sigmoid(x):
    return 1.0 / (1.0 + jnp.exp(-x))


def _gelu_tanh(x):
    return 0.5 * x * (1.0 + jnp.tanh(0.7978845608028654 * (x + 0.044715 * (x * x * x))))


def _log_sigmoid(x):
    return jnp.minimum(x, 0.0) - jnp.log(1.0 + jnp.exp(-jnp.abs(x)))


def _ln(x):
    mu = jnp.mean(x, axis=-1, keepdims=True)
    xc = x - mu
    var = jnp.mean(xc * xc, axis=-1, keepdims=True)
    return xc * lax.rsqrt(var + LN_EPS)


def _dot(a, b):
    return jnp.dot(a, b, preferred_element_type=F32)


def _dot_nt(a, b):
    return lax.dot_general(a, b, (((1,), (1,)), ((), ())), preferred_element_type=F32)


def _dot_tn(a, b):
    return lax.dot_general(a, b, (((0,), (0,)), ((), ())), preferred_element_type=F32)


def _ada_body(c_ref, w_ref, b_ref, o_ref):
    c = c_ref[...]
    s = (c * _sigmoid(c)).astype(BF16)
    o_ref[...] = _dot(s, w_ref[...].astype(BF16)) + b_ref[...]


def _ada_mod(cond_pad, w_ada, b_ada):
    rows, d = cond_pad.shape
    n_out = w_ada.shape[1]
    tn = ADA_COL_TILE
    return pl.pallas_call(
        _ada_body,
        grid=(n_out // tn,),
        in_specs=[pl.BlockSpec((rows, d), lambda j: (0, 0)),
                  pl.BlockSpec((d, tn), lambda j: (0, j)),
                  pl.BlockSpec((1, tn), lambda j: (0, j))],
        out_specs=pl.BlockSpec((rows, tn), lambda j: (0, j)),
        out_shape=jax.ShapeDtypeStruct((rows, n_out), F32),
        compiler_params=_params(1),
        name="ada_mod",
    )(cond_pad, w_ada, b_ada.reshape(1, n_out))


def _segment_specs(segments, tm, d):
    specs, bounds, start = [], [], 0
    for seg in segments:
        tiles = seg.shape[0] // tm
        specs.append(pl.BlockSpec(
            (tm, d), lambda i, *_, s=start, t=tiles: (jnp.clip(i - s, 0, t - 1), 0)))
        bounds.append((start, start + tiles))
        start += tiles
    return specs, bounds


def _in_proj_body(*refs, bounds):
    x_refs = refs[:len(bounds)]
    mod_ref, w_ref, z_ref, h_scr = refs[len(bounds):]
    i = pl.program_id(0)
    for x_ref, (lo, hi) in zip(x_refs, bounds):
        @pl.when((pl.program_id(1) == 0) & (i >= lo) & (i < hi))
        def _():
            m = mod_ref[0]
            h = _ln(x_ref[...]) * (1.0 + m[1:2]) + m[0:1]
            h_scr[...] = h.astype(BF16)

    z_ref[...] = _dot(h_scr[...], w_ref[...])


def _in_proj(x_segments, mod_t, w_in_r):
    d = x_segments[0].shape[1]
    n = sum(seg.shape[0] for seg in x_segments)
    nc = w_in_r.shape[1]
    tm, tn = ROW_TILE, IN_PROJ_COL_TILE
    x_specs, bounds = _segment_specs(x_segments, tm, d)
    return pl.pallas_call(
        functools.partial(_in_proj_body, bounds=bounds),
        grid=(n // tm, nc // tn),
        in_specs=x_specs + [pl.BlockSpec((1, V7X_SUBLANES, d), lambda i, j: (i, 0, 0)),
                            pl.BlockSpec((d, tn), lambda i, j: (0, j))],
        out_specs=pl.BlockSpec((tm, tn), lambda i, j: (i, j)),
        out_shape=jax.ShapeDtypeStruct((n, nc), F32),
        scratch_shapes=[pltpu.VMEM((tm, d), BF16)],
        compiler_params=_params(2),
        name="in_proj",
    )(*x_segments, mod_t, w_in_r)


def _conv3_rows(u, w, shift, period):
    n = u.shape[0]
    pos = lax.broadcasted_iota(jnp.int32, u.shape, 0) % period
    prev = jnp.where(pos >= shift, pltpu.roll(u, shift, axis=0), 0.0)
    nxt = jnp.where(pos < period - shift, pltpu.roll(u, n - shift, axis=0), 0.0)
    return prev * w[0:1] + u * w[1:2] + nxt * w[2:3]


def _conv_body(cb_ref, cc_ref, cx_ref, w_ref, p_ref, *, n_ctx_units, ctx_seq, n_horizontal_tiles):
    u = cc_ref[...] * cx_ref[...]
    w = w_ref[...]
    n = u.shape[0]
    is_ctx = pl.program_id(0) < n_ctx_units
    is_horizontal = pl.program_id(1) < n_horizontal_tiles

    @pl.when(is_ctx)
    def _():
        p_ref[...] = (cb_ref[...] * _conv3_rows(u, w, 1, ctx_seq)).astype(BF16)

    @pl.when(jnp.logical_not(is_ctx) & is_horizontal)
    def _():
        p_ref[...] = (cb_ref[...] * _conv3_rows(u, w, 1, GRID_W)).astype(BF16)

    @pl.when(jnp.logical_not(is_ctx) & jnp.logical_not(is_horizontal))
    def _():
        p_ref[...] = (cb_ref[...] * _conv3_rows(u, w, GRID_W, n)).astype(BF16)


def _conv_mixer(z, w_conv, *, n_ctx, ctx_seq, dec_seq):
    n = z.shape[0]
    d_conv = w_conv.shape[1]
    ct = CONV_CH_TILE
    n_ct = d_conv // ct
    body = functools.partial(_conv_body, n_ctx_units=n_ctx // dec_seq, ctx_seq=ctx_seq,
                             n_horizontal_tiles=n_ct // 2)
    return pl.pallas_call(
        body,
        grid=(n // dec_seq, n_ct),
        in_specs=[pl.BlockSpec((dec_seq, ct), lambda b, c: (b, c)),
                  pl.BlockSpec((dec_seq, ct), lambda b, c: (b, n_ct + c)),
                  pl.BlockSpec((dec_seq, ct), lambda b, c: (b, 2 * n_ct + c)),
                  pl.BlockSpec((3, ct), lambda b, c: (0, c))],
        out_specs=pl.BlockSpec((dec_seq, ct), lambda b, c: (b, c)),
        out_shape=jax.ShapeDtypeStruct((n, d_conv), BF16),
        compiler_params=_params(2),
        name="conv_mixer",
    )(z, z, z, w_conv)


def _split2_dot(tri, g):
    g1 = g.astype(BF16)
    g2 = (g - g1.astype(F32)).astype(BF16)
    return _dot(tri, g1) + _dot(tri, g2)


def _gla_direction(q_ref, k_ref, v_ref, l_ref, wup, bgk, o_ref, s_scr, *, backward, q_scale):
    rows = q_ref.shape[0]
    n_chunks = rows // GLA_CHUNK
    shift = GLA_CHUNK.bit_length() - 1
    t_idx = lax.broadcasted_iota(jnp.int32, (rows, rows), 0)
    s_idx = lax.broadcasted_iota(jnp.int32, (rows, rows), 1)
    t_chunk = jnp.right_shift(t_idx, shift)
    s_chunk = jnp.right_shift(s_idx, shift)
    lag = (s_chunk - t_chunk) if backward else (t_chunk - s_chunk)
    causal = (s_idx >= t_idx) if backward else (s_idx <= t_idx)
    tri = jnp.where(lag == 0, jnp.where(causal, 1.0, 0.0), 0.0).astype(BF16)

    gpre = _dot(l_ref[...].astype(BF16), wup) + bgk
    g = jnp.maximum(_log_sigmoid(gpre) * (1.0 / GATE_NORM), LOG_DECAY_MIN)
    yield
    b = _split2_dot(tri, g)
    yield

    order = list(range(n_chunks - 1, -1, -1)) if backward else list(range(n_chunks))
    scan_pos = {c: j for j, c in enumerate(order)}

    def chunk_total(c):
        row = c * GLA_CHUNK if backward else (c + 1) * GLA_CHUNK - 1
        return b[row:row + 1]

    totals = [chunk_total(c) for c in order]
    prefix = [jnp.zeros_like(totals[0])]
    for t in totals[:-1]:
        prefix.append(prefix[-1] + t)
    total = prefix[-1] + totals[-1]

    def per_chunk_rows(fn):
        return jnp.concatenate(
            [jnp.broadcast_to(fn(scan_pos[c]), (GLA_CHUNK, b.shape[1])) for c in range(n_chunks)],
            axis=0)

    q = q_ref[...] * q_scale
    k = k_ref[...]
    v = v_ref[...].astype(BF16)
    qe = q * jnp.exp(b)
    ke = (k * jnp.exp(-b)).astype(BF16)
    kd = k * jnp.exp(per_chunk_rows(lambda j: totals[j]) - b)
    qe_b = qe.astype(BF16)
    kd_b = kd.astype(BF16)
    yield

    att = jnp.where(lag == 0, jnp.where(causal, _dot_nt(qe_b, ke), 0.0), 0.0)
    for dist in range(1, n_chunks):
        if dist == 1:
            q_dist = qe_b
        else:
            between = per_chunk_rows(
                lambda j: jnp.exp(prefix[j] - prefix[j - dist + 1]) if j >= dist
                else jnp.ones_like(total))
            q_dist = (qe * between).astype(BF16)
        att = jnp.where(lag == dist, _dot_nt(q_dist, kd_b), att)

    q_in = (qe * per_chunk_rows(lambda j: jnp.exp(prefix[j]))).astype(BF16)
    k_out = (kd * per_chunk_rows(lambda j: jnp.exp(total - prefix[j] - totals[j]))).astype(BF16)
    yield
    s = s_scr[...]
    o_ref[...] = _dot(att.astype(BF16), v) + _dot_nt(q_in, s.astype(BF16))
    s_scr[...] = s * jnp.exp(total) + _dot_tn(v, k_out)
    yield


def _gla_body(*refs, zero_init, q_scale):
    if zero_init:
        (qf, kf, vf, lf, qb, kb, vb, lb, wup_ref, bgk_ref,
         of_ref, ob_ref, sfin_ref, sf_scr, sb_scr) = refs
    else:
        (qf, kf, vf, lf, qb, kb, vb, lb, wup_ref, bgk_ref, s0_ref,
         of_ref, ob_ref, sf_scr, sb_scr) = refs
    step = pl.program_id(2)

    @pl.when(step == 0)
    def _():
        if zero_init:
            sf_scr[...] = jnp.zeros_like(sf_scr)
            sb_scr[...] = jnp.zeros_like(sb_scr)
        else:
            sf_scr[...] = s0_ref[0, 0, 0]
            sb_scr[...] = s0_ref[0, 1, 0]

    directions = [
        _gla_direction(qf, kf, vf, lf, wup_ref[0], bgk_ref[0], of_ref, sf_scr,
                       backward=False, q_scale=q_scale),
        _gla_direction(qb, kb, vb, lb, wup_ref[1], bgk_ref[1], ob_ref, sb_scr,
                       backward=True, q_scale=q_scale)]
    for _ in zip(*directions):
        pass

    if zero_init:
        @pl.when(step == pl.num_programs(2) - 1)
        def _():
            sfin_ref[0, 0, 0] = sf_scr[...]
            sfin_ref[0, 1, 0] = sb_scr[...]


def _gla(z, wup_pad, bgk, s0_t, *, row0, n_seq, seq_len, d_k, d_v, cols):
    dk, dv = d_k // GLA_HEADS, d_v // GLA_HEADS
    rs = min(GLA_STEP_ROWS, seq_len)
    steps = seq_len // rs
    blk0 = row0 // rs
    q_blk, k_blk, v_blk, l_blk = (cols["q"] // dk, cols["k"] // dk, cols["v"] // dv,
                                  cols["lfb"] // V7X_LANES)
    zero_init = s0_t is None

    def fwd(col_blk):
        return lambda b, h, s: (blk0 + b * steps + s, col_blk(h))

    def bwd(col_blk):
        return lambda b, h, s: (blk0 + b * steps + (steps - 1 - s), col_blk(h))

    def dir_specs(mk):
        return [pl.BlockSpec((rs, dk), mk(lambda h: q_blk + h)),
                pl.BlockSpec((rs, dk), mk(lambda h: k_blk + h)),
                pl.BlockSpec((rs, dv), mk(lambda h: v_blk + h)),
                pl.BlockSpec((rs, V7X_LANES), mk(lambda h: l_blk))]

    in_specs = dir_specs(fwd) + dir_specs(bwd) + [
        pl.BlockSpec((2, V7X_LANES, dk), lambda b, h, s: (0, 0, h)),
        pl.BlockSpec((2, 1, dk), lambda b, h, s: (0, 0, h))]
    args = [z] * 8 + [wup_pad, bgk]
    n_rows = n_seq * seq_len
    out_shape = [jax.ShapeDtypeStruct((n_rows, d_v), F32)] * 2
    out_specs = [pl.BlockSpec((rs, dv), lambda b, h, s: (b * steps + s, h)),
                 pl.BlockSpec((rs, dv), lambda b, h, s: (b * steps + (steps - 1 - s), h))]
    state_spec = pl.BlockSpec((1, 2, 1, dv, dk), lambda b, h, s: (b, 0, h, 0, 0))
    if zero_init:
        out_shape.append(jax.ShapeDtypeStruct((n_seq, 2, GLA_HEADS, dv, dk), F32))
        out_specs.append(state_spec)
    else:
        in_specs.append(state_spec)
        args.append(s0_t)
    body = functools.partial(_gla_body, zero_init=zero_init, q_scale=float(dk) ** -0.5)
    return pl.pallas_call(
        body,
        grid=(n_seq, GLA_HEADS, steps),
        in_specs=in_specs,
        out_specs=out_specs,
        out_shape=out_shape,
        scratch_shapes=[pltpu.VMEM((dv, dk), F32), pltpu.VMEM((dv, dk), F32)],
        compiler_params=_params(3),
        name="gla_context" if zero_init else "gla_latent",
    )(*args)


def _gla_post_body(*refs, bounds):
    n_seg = len(bounds)
    of_refs, ob_refs = refs[:n_seg], refs[n_seg:2 * n_seg]
    r_ref, w_ref, o_ref = refs[2 * n_seg:]
    i = pl.program_id(0)
    for of_ref, ob_ref, (lo, hi) in zip(of_refs, ob_refs, bounds):
        @pl.when((i >= lo) & (i < hi))
        def _():
            o = of_ref[...] + ob_ref[...]
            o = o * lax.rsqrt(jnp.mean(o * o, axis=-1, keepdims=True) + LN_EPS) * w_ref[...]
            r = r_ref[...]
            o_ref[...] = (o * (r * _sigmoid(r))).astype(BF16)


def _gla_post(o_f_segments, o_b_segments, z, w_norm, *, r_col):
    d_v = o_f_segments[0].shape[1]
    n = sum(seg.shape[0] for seg in o_f_segments)
    dv = d_v // GLA_HEADS
    tm = ROW_TILE
    r_blk = r_col // dv
    bounds, seg_specs, start = [], [], 0
    for seg in o_f_segments:
        tiles = seg.shape[0] // tm
        seg_specs.append(pl.BlockSpec(
            (tm, dv), lambda i, h, s=start, t=tiles: (
                jnp.clip(i - s, 0, t - 1), jnp.where((i >= s) & (i < s + t), h, 0))))
        bounds.append((start, start + tiles))
        start += tiles
    return pl.pallas_call(
        functools.partial(_gla_post_body, bounds=bounds),
        grid=(n // tm, GLA_HEADS),
        in_specs=seg_specs + seg_specs + [
            pl.BlockSpec((tm, dv), lambda i, h: (i, r_blk + h)),
            pl.BlockSpec((1, dv), lambda i, h: (0, h))],
        out_specs=pl.BlockSpec((tm, dv), lambda i, h: (i, h)),
        out_shape=jax.ShapeDtypeStruct((n, d_v), BF16),
        compiler_params=_params(2),
        name="gla_post",
    )(*o_f_segments, *o_b_segments, z, w_norm.reshape(1, d_v))


def _merge_body(p_ref, og_ref, ga_ref, gb_ref, wa_ref, wb_ref, y_ref):
    y_a = _dot(p_ref[...], wa_ref[...])
    y_b = _dot(og_ref[...], wb_ref[...])
    y_ref[...] = (_sigmoid(ga_ref[...]) * y_a + _sigmoid(gb_ref[...]) * y_b).astype(BF16)


def _merge(p, og, z, w_a, w_b, *, ga_col, gb_col):
    n, d_conv = p.shape
    d_v = og.shape[1]
    d = w_a.shape[1]
    tm, tn = ROW_TILE, MERGE_COL_TILE
    ga_blk, gb_blk = ga_col // tn, gb_col // tn
    return pl.pallas_call(
        _merge_body,
        grid=(n // tm, d // tn),
        in_specs=[pl.BlockSpec((tm, d_conv), lambda i, j: (i, 0)),
                  pl.BlockSpec((tm, d_v), lambda i, j: (i, 0)),
                  pl.BlockSpec((tm, tn), lambda i, j: (i, ga_blk + j)),
                  pl.BlockSpec((tm, tn), lambda i, j: (i, gb_blk + j)),
                  pl.BlockSpec((d_conv, tn), lambda i, j: (0, j)),
                  pl.BlockSpec((d_v, tn), lambda i, j: (0, j))],
        out_specs=pl.BlockSpec((tm, tn), lambda i, j: (i, j)),
        out_shape=jax.ShapeDtypeStruct((n, d), BF16),
        compiler_params=_params(2),
        name="merge",
    )(p, og, z, z, w_a, w_b)


def _out_proj_body(*refs, alpha, bounds):
    x_refs = refs[:len(bounds)]
    y_ref, mod_ref, wo_ref, lg_ref, lb_ref, x1_ref, ht_ref = refs[len(bounds):]
    m = mod_ref[0]
    mix = m[2:3] * _dot(y_ref[...], wo_ref[...])
    i = pl.program_id(0)
    for x_ref, (lo, hi) in zip(x_refs, bounds):
        @pl.when((i >= lo) & (i < hi))
        def _():
            x1 = _ln(alpha * x_ref[...] + mix) * lg_ref[...] + lb_ref[...]
            x1_ref[...] = x1
            h = _ln(x1) * (1.0 + m[4:5]) + m[3:4]
            ht_ref[...] = h.T.astype(BF16)


def _out_proj(y, x_segments, mod_t, w_o, ln_g, ln_b, *, alpha):
    n, d = y.shape
    tm = ROW_TILE
    x_specs, bounds = _segment_specs(x_segments, tm, d)
    return pl.pallas_call(
        functools.partial(_out_proj_body, alpha=alpha, bounds=bounds),
        grid=(n // tm,),
        in_specs=x_specs + [
                  pl.BlockSpec((tm, d), lambda i: (i, 0)),
                  pl.BlockSpec((1, V7X_SUBLANES, d), lambda i: (i, 0, 0)),
                  pl.BlockSpec((d, d), lambda i: (0, 0)),
                  pl.BlockSpec((1, d), lambda i: (0, 0)),
                  pl.BlockSpec((1, d), lambda i: (0, 0))],
        out_specs=[pl.BlockSpec((tm, d), lambda i: (i, 0)),
                   pl.BlockSpec((d, tm), lambda i: (0, i))],
        out_shape=[jax.ShapeDtypeStruct((n, d), F32),
                   jax.ShapeDtypeStruct((d, n), BF16)],
        compiler_params=_params(1),
        name="out_proj",
    )(*x_segments, y, mod_t, w_o, ln_g.reshape(1, d), ln_b.reshape(1, d))


def _sorting_network(n):
    pairs, p = [], 1
    while p < n:
        k = p
        while k >= 1:
            for j in range(k % p, n - k, 2 * k):
                for i in range(min(k, n - j - k)):
                    if (i + j) // (2 * p) == (i + j + k) // (2 * p):
                        pairs.append((i + j, i + j + k))
            k //= 2
        p *= 2
    return pairs


def _bitonic_merge_network(n):
    pairs, k = [], n // 2
    while k >= 1:
        pairs += [(i, i + k) for i in range(n) if (i // k) % 2 == 0]
        k //= 2
    return pairs


def _compare_exchange(x, pairs):
    for i, j in pairs:
        x[i], x[j] = jnp.maximum(x[i], x[j]), jnp.minimum(x[i], x[j])
    return x


def _top_sorted(s):
    sub = V7X_SUBLANES
    m = s.shape[0] // sub
    out = []
    for lane0 in range(0, s.shape[1], V7X_LANES):
        lanes = slice(lane0, lane0 + V7X_LANES)
        x = _compare_exchange([s[sub * k:sub * (k + 1), lanes] for k in range(m)],
                              _sorting_network(m))
        shift = sub // 2
        while shift >= 1:
            partner = [pltpu.roll(v, shift, axis=0) for v in x]
            x = _compare_exchange([jnp.maximum(x[k], partner[m - 1 - k]) for k in range(m)],
                                  _bitonic_merge_network(m))
            shift //= 2
        out.append(jnp.concatenate([v[0:1] for v in x], axis=0))
    return jnp.concatenate(out, axis=1)


def _router_body(ht_ref, wpq_ref, keys_ref, a1_ref, r_ref, a2_ref, *, half_dim):
    ht = ht_ref[...]
    k1 = keys_ref[0]
    k2 = keys_ref[1]
    k = PEER_TOPK
    sub = V7X_SUBLANES

    def candidates(v1, v2):
        groups = [v1[0:1] * v2[0:sub], v1[0:1] * v2[sub:k]]
        groups += [v1[a:a + 1] * v2[0:sub] for a in range(1, sub)]
        groups.append(v1[sub:k] * v2[0:1])
        return jnp.concatenate(groups, axis=0)

    for h in range(PEER_HEADS):
        qq = _dot(wpq_ref[h * 2 * half_dim:(h + 1) * 2 * half_dim, :], ht)
        q1 = qq[:half_dim].astype(BF16)
        q2 = qq[half_dim:].astype(BF16)
        s1 = _dot(k1, q1)
        s2 = _dot(k2, q2)
        t1 = _top_sorted(s1)
        t2 = _top_sorted(s2)
        a1 = jnp.exp(s1 - t1[0:1])
        a2 = jnp.exp(s2 - t2[0:1])
        v1 = jnp.exp(t1 - t1[0:1])
        v2 = jnp.exp(t2 - t2[0:1])
        cand = candidates(v1, v2)
        c = cand
        for _ in range(k - 1):
            m = jnp.max(c, axis=0, keepdims=True)
            c = jnp.where(c == m, -1.0, c)
        tau = jnp.maximum(jnp.max(c, axis=0, keepdims=True), 0.0)
        sel = cand >= tau
        inv_z = 1.0 / jnp.sum(jnp.where(sel, cand, 0.0), axis=0, keepdims=True)
        theta = jnp.min(jnp.where(sel, candidates(v1, v2 * inv_z), jnp.inf), axis=0, keepdims=True)
        a1_ref[h] = a1
        r_ref[h] = (theta * (1.0 - 2.0 ** -20)) / a1
        a2_ref[h] = (a2 * inv_z).astype(BF16)


def _router(ht, wpq_t, keys):
    d, n = ht.shape
    n_keys, half_dim = keys.shape[1], keys.shape[2]
    assert n_keys == PEER_TOPK * V7X_SUBLANES
    tt = ROUTER_TOK_TILE
    blk = pl.BlockSpec((PEER_HEADS, n_keys, tt), lambda i: (0, 0, i))
    return pl.pallas_call(
        functools.partial(_router_body, half_dim=half_dim),
        grid=(n // tt,),
        in_specs=[pl.BlockSpec((d, tt), lambda i: (0, i)),
                  pl.BlockSpec(wpq_t.shape, lambda i: (0, 0)),
                  pl.BlockSpec(keys.shape, lambda i: (0, 0, 0))],
        out_specs=[blk, blk, blk],
        out_shape=[jax.ShapeDtypeStruct((PEER_HEADS, n_keys, n), F32),
                   jax.ShapeDtypeStruct((PEER_HEADS, n_keys, n), F32),
                   jax.ShapeDtypeStruct((PEER_HEADS, n_keys, n), BF16)],
        compiler_params=_params(1),
        name="peer_router",
    )(ht, wpq_t, keys)


def _experts_body(ht_ref, a1_ref, r_ref, a2_ref, wu_ref, wvt_ref, ft_ref, gc_scr, *, n_keys):
    @pl.when(pl.program_id(1) == 0)
    def _():
        ft_ref[...] = jnp.zeros_like(ft_ref)

    zero = jnp.zeros((), BF16)
    u = _dot(wu_ref[...], ht_ref[...])
    for ii in range(u.shape[0] // n_keys):
        rows = slice(ii * n_keys, (ii + 1) * n_keys)
        coef = None
        for h in range(PEER_HEADS):
            a2 = a2_ref[h]
            bound = r_ref[h, ii:ii + 1, :].astype(BF16)
            weight = a1_ref[h, ii:ii + 1, :].astype(BF16)
            term = jnp.where(a2 >= bound, a2, zero) * weight
            coef = term if coef is None else coef + term
        gc_scr[rows, :] = coef * _gelu_tanh(u[rows]).astype(BF16)
    ft_ref[...] += _dot(wvt_ref[...], gc_scr[...])


def _experts(ht, a1, r, a2, w_u, w_vt):
    d, n = ht.shape
    n_exp = w_u.shape[0]
    n_keys = a1.shape[1]
    tt, te = PEER_TOK_TILE, PEER_EXPERT_TILE
    rows_per_step = te // n_keys
    row_blk = pl.BlockSpec((PEER_HEADS, rows_per_step, tt), lambda i, e: (0, e, i))
    return pl.pallas_call(
        functools.partial(_experts_body, n_keys=n_keys),
        grid=(n // tt, n_exp // te),
        in_specs=[pl.BlockSpec((d, tt), lambda i, e: (0, i)),
                  row_blk, row_blk,
                  pl.BlockSpec((PEER_HEADS, n_keys, tt), lambda i, e: (0, 0, i)),
                  pl.BlockSpec((te, d), lambda i, e: (e, 0)),
                  pl.BlockSpec((d, te), lambda i, e: (0, e))],
        out_specs=pl.BlockSpec((d, tt), lambda i, e: (0, i)),
        out_shape=jax.ShapeDtypeStruct((d, n), F32),
        scratch_shapes=[pltpu.VMEM((te, tt), BF16)],
        compiler_params=_params(2),
        name="peer_experts",
    )(ht, a1, r, a2, w_u, w_vt)


def _ffn_out_body(ft_ref, x1_ref, mod_ref, lg_ref, lb_ref, *o_refs, alpha, bounds):
    m = mod_ref[0]
    f = ft_ref[...].T
    out = _ln(alpha * x1_ref[...] + m[5:6] * f) * lg_ref[...] + lb_ref[...]
    i = pl.program_id(0)
    for o_ref, (lo, hi) in zip(o_refs, bounds):
        @pl.when((i >= lo) & (i < hi))
        def _():
            o_ref[...] = out


def _ffn_out(ft, x1, mod_t, ln_g, ln_b, *, alpha, out_rows):
    n, d = x1.shape
    tm = ROW_TILE
    out_specs, out_shape, bounds, start = [], [], [], 0
    for rows in out_rows:
        tiles = rows // tm
        out_specs.append(pl.BlockSpec(
            (tm, d), lambda i, s=start, t=tiles: (jnp.clip(i - s, 0, t - 1), 0)))
        out_shape.append(jax.ShapeDtypeStruct((rows, d), F32))
        bounds.append((start, start + tiles))
        start += tiles
    return pl.pallas_call(
        functools.partial(_ffn_out_body, alpha=alpha, bounds=bounds),
        grid=(n // tm,),
        in_specs=[pl.BlockSpec((d, tm), lambda i: (0, i)),
                  pl.BlockSpec((tm, d), lambda i: (i, 0)),
                  pl.BlockSpec((1, V7X_SUBLANES, d), lambda i: (i, 0, 0)),
                  pl.BlockSpec((1, d), lambda i: (0, 0)),
                  pl.BlockSpec((1, d), lambda i: (0, 0))],
        out_specs=out_specs,
        out_shape=out_shape,
        compiler_params=_params(1),
        name="ffn_out",
    )(ft, x1, mod_t, ln_g.reshape(1, d), ln_b.reshape(1, d))


def kernel(x_prompt, x_sample, state_gla, c, c_ctx, w_in, w_conv, w_a, w_gk_up, b_gk, w_gla_norm,
           w_b, w_o, w_ada, b_ada, ln_g, ln_b, w_pq, peer_keys, peer_u, peer_v):
    n_b, seq, d = x_prompt.shape
    dec_b, dec_seq, _ = x_sample.shape
    depth = w_in.shape[0]
    d_conv = w_conv.shape[2]
    d_k = b_gk.shape[2]
    d_v = w_b.shape[1]
    gk_rank = w_gk_up.shape[2]
    n_ctx, n_lat = n_b * seq, dec_b * dec_seq
    n = n_ctx + n_lat
    alpha = (2.0 * depth) ** 0.25
    assert n_ctx % ROW_TILE == 0 and dec_seq % ROW_TILE == 0 and n_ctx % dec_seq == 0
    assert seq % GLA_CHUNK == 0 and dec_seq % GLA_STEP_ROWS == 0 and dec_seq % GRID_W == 0
    assert dec_seq % seq == 0 and 2 * gk_rank <= V7X_LANES

    main = 3 * d_conv + 2 * d_k + 2 * d_v
    cols = {"cb": 0, "q": 3 * d_conv, "k": 3 * d_conv + d_k, "v": 3 * d_conv + 2 * d_k,
            "r": 3 * d_conv + 2 * d_k + d_v, "ga": main, "gb": main + d, "lfb": main + 2 * d}
    n_cols = cols["lfb"] + V7X_LANES
    assert n_cols % IN_PROJ_COL_TILE == 0

    x_segments = [x_prompt.reshape(n_ctx, d), x_sample.reshape(n_lat, d)]
    cond = jnp.concatenate([c_ctx[None, :], c], axis=0)
    cond_rows = -(-cond.shape[0] // V7X_SUBLANES) * V7X_SUBLANES
    cond_pad = jnp.pad(cond, ((0, cond_rows - cond.shape[0]), (0, 0)))
    tile_row0 = jnp.arange(n // ROW_TILE) * ROW_TILE
    tile_cond = jnp.where(tile_row0 < n_ctx, 0, 1 + (tile_row0 - n_ctx) // dec_seq)

    ctx_states = []
    for l in range(depth):
        w = w_in[l]
        w_in_r = jnp.concatenate(
            [w[:, :main], w[:, main + 2 * gk_rank:], w[:, main:main + 2 * gk_rank],
             jnp.zeros((d, V7X_LANES - 2 * gk_rank), w.dtype)], axis=1).astype(BF16)
        wup_pad = jnp.zeros((2, V7X_LANES, d_k), F32)
        wup_pad = wup_pad.at[0, :gk_rank].set(w_gk_up[l, 0]).at[1, gk_rank:2 * gk_rank].set(w_gk_up[l, 1])
        wup_pad = wup_pad.astype(BF16)
        bgk = b_gk[l].reshape(2, 1, d_k)

        mod = _ada_mod(cond_pad, w_ada[l], b_ada[l]).reshape(cond_rows, N_ADA, d)
        mod_t = jnp.pad(mod[tile_cond], ((0, 0), (0, V7X_SUBLANES - N_ADA), (0, 0)))

        z = _in_proj(x_segments, mod_t, w_in_r)
        p = _conv_mixer(z, w_conv[l], n_ctx=n_ctx, ctx_seq=seq, dec_seq=dec_seq)

        of_c, ob_c, s_fin = _gla(z, wup_pad, bgk, None, row0=0, n_seq=n_b, seq_len=seq,
                                 d_k=d_k, d_v=d_v, cols=cols)
        s0_t = jnp.swapaxes(state_gla[:, l], -1, -2)
        of_l, ob_l = _gla(z, wup_pad, bgk, s0_t, row0=n_ctx, n_seq=dec_b, seq_len=dec_seq,
                          d_k=d_k, d_v=d_v, cols=cols)
        ctx_states.append(jnp.swapaxes(s_fin, -1, -2))
        og = _gla_post([of_c, of_l], [ob_c, ob_l], z, w_gla_norm[l], r_col=cols["r"])

        y = _merge(p, og, z, w_a[l].astype(BF16), w_b[l].astype(BF16),
                   ga_col=cols["ga"], gb_col=cols["gb"])
        x1, ht = _out_proj(y, x_segments, mod_t, w_o[l].astype(BF16), ln_g[l, 0], ln_b[l, 0],
                           alpha=alpha)

        a1, r, a2 = _router(ht, w_pq[l].T.astype(BF16), peer_keys[l].astype(BF16))
        ft = _experts(ht, a1, r, a2, peer_u[l].astype(BF16), peer_v[l].T.astype(BF16))
        last = l == depth - 1
        x_segments = _ffn_out(ft, x1, mod_t, ln_g[l, 1], ln_b[l, 1], alpha=alpha,
                              out_rows=[n_ctx, n_lat] if last else [n])

    new_state = jnp.stack(ctx_states, axis=1).astype(x_prompt.dtype)
    xp, xs = x_segments
    return (xp.reshape(n_b, seq, d), xs.reshape(dec_b, dec_seq, d), new_state)
```
